```python
import functools
import jax
import jax.numpy as jnp
from jax import lax
import numpy as np


D_MODEL = 1024
BATCH = 2
SEQ = 8192
DEPTH = 4
DEC_BATCH = 128
DEC_SEQ = 4
PAST_LEN = 2048
PAGE_SIZE = 128

POOL_WIDTH = D_MODEL // 2
POOL_WINDOWS = (2, 4, 8, 16)
N_POOL_GROUPS = len(POOL_WINDOWS)
POOL_GROUP = POOL_WIDTH // N_POOL_GROUPS
POOL_STATE = max(POOL_WINDOWS) - 1
N_HEADS = 8
HEAD_DIM = 64
ATTN_WIDTH = N_HEADS * HEAD_DIM
N_IDX_HEADS = 8
IDX_DIM = 64
TOPK_MAX = 256
Q_BLOCK = 128
N_BUCKETS = 32
MAX_DISTANCE = 128
MIX_WIDTH = POOL_WIDTH + ATTN_WIDTH
IN_SPLITS = (POOL_WIDTH, ATTN_WIDTH, ATTN_WIDTH, ATTN_WIDTH, N_IDX_HEADS * IDX_DIM, IDX_DIM, N_IDX_HEADS)
IN_WIDTH = sum(IN_SPLITS)
D_FF = 2816
CONV_WIDTH = 3
EPS = 1e-6

kernel_name = "hymba_pool_dsa_convffn_decoder_step"


def _rmsnorm(x, g):
    xf = x.astype(jnp.float32)
    y = xf * lax.rsqrt(jnp.mean(xf * xf, axis=-1, keepdims=True) + EPS)
    return (y * g.astype(jnp.float32)).astype(x.dtype)


def _gather_rows(rows, idx):
    return jax.vmap(lambda r, i: r[i])(rows, idx)


def _t5_bucket(n):
    max_exact = N_BUCKETS // 2
    nf = jnp.maximum(n, 1).astype(jnp.float32)
    large = max_exact + (jnp.log(nf / max_exact) / np.float32(np.log(MAX_DISTANCE / max_exact))
                         * (N_BUCKETS - max_exact)).astype(jnp.int32)
    large = jnp.minimum(large, N_BUCKETS - 1)
    return jnp.where(n < max_exact, n, large)


def _project(h, w_in):
    b, t = h.shape[:2]
    z = h @ w_in
    offs = [int(o) for o in np.cumsum(IN_SPLITS)[:-1]]
    u, q, k, v, qi, ki, wi = jnp.split(z, offs, axis=-1)
    q = q.reshape(b, t, N_HEADS, HEAD_DIM)
    k = k.reshape(b, t, N_HEADS, HEAD_DIM)
    v = v.reshape(b, t, N_HEADS, HEAD_DIM)
    qi = qi.reshape(b, t, N_IDX_HEADS, IDX_DIM) * (IDX_DIM ** -0.5)
    wi = wi * (N_IDX_HEADS ** -0.5)
    return u, q, k, v, qi, ki, wi


def _pool_mixer(u, prev, pos0, w_pool, pool_scale):
    b, t, _ = u.shape
    ext = jnp.concatenate([prev.astype(u.dtype), u], axis=1)
    cs = jnp.cumsum(ext.astype(jnp.float32), axis=1)
    cs = jnp.concatenate([jnp.zeros((b, 1, POOL_WIDTH), jnp.float32), cs], axis=1)
    pos = pos0 + jnp.arange(t)
    means = []
    for g, w in enumerate(POOL_WINDOWS):
        sl = slice(g * POOL_GROUP, (g + 1) * POOL_GROUP)
        s = cs[:, POOL_STATE + 1:, sl] - cs[:, POOL_STATE + 1 - w: POOL_STATE + 1 - w + t, sl]
        cnt = jnp.minimum(w, pos + 1).astype(jnp.float32)
        means.append(s / cnt[None, :, None])
    d = (jnp.concatenate(means, axis=-1) - u.astype(jnp.float32)).reshape(b, t, N_POOL_GROUPS, POOL_GROUP)
    y = jnp.einsum('btgc,gcd->btgd', d, w_pool.astype(jnp.float32)).reshape(b, t, POOL_WIDTH)
    y = y * pool_scale.astype(jnp.float32)
    return y.astype(u.dtype), ext[:, -POOL_STATE:]


def _indexer_select(qi, wi, ki, qpos, topk):
    s = jnp.einsum('bqhd,bkd->bqhk', qi.astype(jnp.float32), ki.astype(jnp.float32))
    score = jnp.einsum('bqh,bqhk->bqk', wi.astype(jnp.float32), jax.nn.relu(s))
    admissible = jnp.arange(ki.shape[1])[None, :] <= qpos[:, None]
    score = jnp.where(admissible[None], score, -jnp.inf)
    _, sel = lax.top_k(score, topk)
    valid = sel <= qpos[None, :, None]
    return sel, valid


def _sparse_attend(q, ks, vs, sel, valid, qpos, rel_bias):
    logits = jnp.einsum('bqhd,bqkhd->bqhk', q.astype(jnp.float32), ks.astype(jnp.float32)) * (HEAD_DIM ** -0.5)
    dist = jnp.maximum(qpos[None, :, None] - sel, 0)
    bias = rel_bias.astype(jnp.float32)[_t5_bucket(dist)]
    logits = logits + jnp.swapaxes(bias, -1, -2)
    logits = jnp.where(valid[:, :, None, :], logits, -jnp.inf)
    p = jax.nn.softmax(logits, axis=-1)
    out = jnp.einsum('bqhk,bqkhd->bqhd', p, vs.astype(jnp.float32))
    return out.astype(q.dtype)


def _prompt_attention(q, k, v, qi, ki, wi, rel_bias):
    b, t = q.shape[:2]
    topk = min(TOPK_MAX, t // 4)
    n_blk = t // Q_BLOCK

    def one_block(i):
        start = i * Q_BLOCK
        qb = lax.dynamic_slice_in_dim(q, start, Q_BLOCK, axis=1)
        qib = lax.dynamic_slice_in_dim(qi, start, Q_BLOCK, axis=1)
        wib = lax.dynamic_slice_in_dim(wi, start, Q_BLOCK, axis=1)
        qpos = start + jnp.arange(Q_BLOCK)
        sel, valid = _indexer_select(qib, wib, ki, qpos, topk)
        return _sparse_attend(qb, _gather_rows(k, sel), _gather_rows(v, sel), sel, valid, qpos, rel_bias)

    out = lax.map(one_block, jnp.arange(n_blk))
    return jnp.moveaxis(out, 0, 1).reshape(b, t, N_HEADS, HEAD_DIM)


def _sample_attention(q, k, v, qi, ki, wi, layer, cache_k, cache_v, cache_idx_k, page_table, rel_bias):
    db, tq = q.shape[:2]
    past_len = page_table.shape[1] * PAGE_SIZE
    ki_past = cache_idx_k[layer, page_table].reshape(db, past_len, IDX_DIM)
    ki_all = jnp.concatenate([ki_past.astype(ki.dtype), ki], axis=1)
    qpos = past_len + jnp.arange(tq)
    topk = min(TOPK_MAX, (past_len + tq) // 4)
    sel, valid = _indexer_select(qi, wi, ki_all, qpos, topk)
    in_past = sel < past_len
    past_idx = jnp.minimum(sel, past_len - 1)
    phys = _gather_rows(page_table, past_idx // PAGE_SIZE)
    off = past_idx % PAGE_SIZE
    new_idx = jnp.clip(sel - past_len, 0, tq - 1)

    def pick(cache, new):
        past = cache[layer, phys, off].astype(new.dtype)
        return jnp.where(in_past[..., None, None], past, _gather_rows(new, new_idx))

    return _sparse_attend(q, pick(cache_k, k), pick(cache_v, v), sel, valid, qpos, rel_bias)


def _conv_ffn(h, prev, w_up, conv_w, conv_b, w_down):
    t = h.shape[1]
    up = h @ w_up
    ext = jnp.concatenate([prev.astype(up.dtype), up], axis=1)
    y = conv_b
    for j in range(CONV_WIDTH):
        y = y + conv_w[j] * ext[:, j:j + t]
    a, g = jnp.split(y, 2, axis=-1)
    return (jax.nn.silu(g) * a) @ w_down, ext[:, -(CONV_WIDTH - 1):]


def _layer(x, c, layer, attend, pool_prev, conv_prev, pos0, w_mod, b_mod, norm_attn_g, norm_ffn_g,
           w_in, w_pool, pool_scale, w_out, w_up, conv_w, conv_b, w_down):
    b, t = x.shape[:2]
    mod = jax.nn.silu(c) @ w_mod[layer] + b_mod[layer]
    sh_a, sc_a, g_a, sh_f, sc_f, g_f = [m[:, None, :] for m in jnp.split(mod, 6, axis=-1)]
    h = _rmsnorm(x, norm_attn_g[layer]) * (1 + sc_a) + sh_a
    u, q, k, v, qi, ki, wi = _project(h, w_in[layer])
    pool_out, pool_state = _pool_mixer(u, pool_prev, pos0, w_pool[layer], pool_scale[layer])
    attn_out = attend(q, k, v, qi, ki, wi).reshape(b, t, ATTN_WIDTH)
    mixed = jnp.concatenate([pool_out, attn_out], axis=-1) @ w_out[layer]
    x = x + g_a * mixed
    h = _rmsnorm(x, norm_ffn_g[layer]) * (1 + sc_f) + sh_f
    f, conv_state = _conv_ffn(h, conv_prev, w_up[layer], conv_w[layer], conv_b[layer], w_down[layer])
    x = x + g_f * f
    return x, (k, v, ki, pool_state, conv_state)


def setup_inputs(seed: int = 0) -> dict:
    key = jax.random.key(seed)
    ks = jax.random.split(key, 24)
    f32 = jnp.float32
    n_pages = PAST_LEN // PAGE_SIZE
    n_used = DEC_BATCH * n_pages
    n_pool = n_used + n_used // 4

    def nrm(k, shape, s):
        return jax.random.normal(k, shape, f32) * s

    page_table = jax.random.permutation(ks[0], n_pool)[:n_used].reshape(DEC_BATCH, n_pages).astype(jnp.int32)
    return {
        "x_prompt": nrm(ks[1], (BATCH, SEQ, D_MODEL), 1.0),
        "x_sample": nrm(ks[2], (DEC_BATCH, DEC_SEQ, D_MODEL), 1.0),
        "c_prompt": nrm(ks[3], (BATCH, D_MODEL), 1.0),
        "c_sample": nrm(ks[4], (DEC_BATCH, D_MODEL), 1.0),
        "cache_k": nrm(ks[5], (DEPTH, n_pool, PAGE_SIZE, N_HEADS, HEAD_DIM), 1.0),
        "cache_v": nrm(ks[6], (DEPTH, n_pool, PAGE_SIZE, N_HEADS, HEAD_DIM), 1.0),
        "cache_idx_k": nrm(ks[7], (DEPTH, n_pool, PAGE_SIZE, IDX_DIM), 1.0),
        "state_pool": nrm(ks[8], (DEPTH, DEC_BATCH, POOL_STATE, POOL_WIDTH), 1.0),
        "state_conv": nrm(ks[9], (DEPTH, DEC_BATCH, CONV_WIDTH - 1, 2 * D_FF), 1.0),
        "page_table": page_table,
        "w_mod": nrm(ks[10], (DEPTH, D_MODEL, 6 * D_MODEL), 0.5 * D_MODEL ** -0.5),
        "b_mod": nrm(ks[11], (DEPTH, 6 * D_MODEL), 0.01),
        "norm_attn_g": 1.0 + nrm(ks[12], (DEPTH, D_MODEL), 0.05),
        "norm_ffn_g": 1.0 + nrm(ks[13], (DEPTH, D_MODEL), 0.05),
        "w_in": nrm(ks[14], (DEPTH, D_MODEL, IN_WIDTH), D_MODEL ** -0.5),
        "w_pool": nrm(ks[15], (DEPTH, N_POOL_GROUPS, POOL_GROUP, POOL_GROUP), POOL_GROUP ** -0.5),
        "pool_scale": 1.0 + nrm(ks[16], (DEPTH, POOL_WIDTH), 0.1),
        "rel_bias": nrm(ks[17], (N_BUCKETS, N_HEADS), 0.5),
        "w_out": nrm(ks[18], (DEPTH, MIX_WIDTH, D_MODEL), MIX_WIDTH ** -0.5),
        "w_up": nrm(ks[19], (DEPTH, D_MODEL, 2 * D_FF), D_MODEL ** -0.5),
        "conv_w": nrm(ks[20], (DEPTH, CONV_WIDTH, 2 * D_FF), CONV_WIDTH ** -0.5),
        "conv_b": nrm(ks[21], (DEPTH, 2 * D_FF), 0.01),
        "w_down": nrm(ks[22], (DEPTH, D_FF, D_MODEL), D_FF ** -0.5),
        "final_norm_g": 1.0 + nrm(ks[23], (D_MODEL,), 0.05),
    }


def reference(x_prompt, x_sample, c_prompt, c_sample, cache_k, cache_v, cache_idx_k, state_pool, state_conv,
              page_table, w_mod, b_mod, norm_attn_g, norm_ffn_g, w_in, w_pool, pool_scale, rel_bias, w_out,
              w_up, conv_w, conv_b, w_down, final_norm_g):
    weights = dict(w_mod=w_mod, b_mod=b_mod, norm_attn_g=norm_attn_g, norm_ffn_g=norm_ffn_g, w_in=w_in,
                   w_pool=w_pool, pool_scale=pool_scale, w_out=w_out, w_up=w_up, conv_w=conv_w,
                   conv_b=conv_b, w_down=w_down)
    past_len = page_table.shape[1] * PAGE_SIZE
    bp = x_prompt.shape[0]
    xp, xs = x_prompt, x_sample
    st_p, st_s = [], []
    prompt_attend = functools.partial(_prompt_attention, rel_bias=rel_bias)
    for l in range(DEPTH):
        pool0 = jnp.zeros((bp, POOL_STATE, POOL_WIDTH), xp.dtype)
        conv0 = jnp.zeros((bp, CONV_WIDTH - 1, 2 * D_FF), xp.dtype)
        xp, sp = _layer(xp, c_prompt, l, prompt_attend, pool0, conv0, 0, **weights)
        sample_attend = functools.partial(_sample_attention, layer=l, cache_k=cache_k, cache_v=cache_v,
                                          cache_idx_k=cache_idx_k, page_table=page_table, rel_bias=rel_bias)
        xs, ss = _layer(xs, c_sample, l, sample_attend, state_pool[l], state_conv[l], past_len, **weights)
        st_p.append(sp)
        st_s.append(ss)
    y_prompt = _rmsnorm(xp, final_norm_g)
    y_sample = _rmsnorm(xs, final_norm_g)
    k_prompt = jnp.stack([s[0] for s in st_p])
    v_prompt = jnp.stack([s[1] for s in st_p])
    idx_k_prompt = jnp.stack([s[2] for s in st_p])
    pool_prompt = jnp.stack([s[3] for s in st_p])
    conv_prompt = jnp.stack([s[4] for s in st_p])
    k_sample = jnp.stack([s[0] for s in st_s])
    v_sample = jnp.stack([s[1] for s in st_s])
    idx_k_sample = jnp.stack([s[2] for s in st_s])
    pool_sample = jnp.stack([s[3] for s in st_s])
    conv_sample = jnp.stack([s[4] for s in st_s])
    return (y_prompt, y_sample, k_prompt, v_prompt, idx_k_prompt, pool_prompt, conv_prompt,
            k_sample, v_sample, idx_k_sample, pool_sample, conv_sample)
```

```python
import functools

import numpy as np
import jax
import jax.numpy as jnp
from jax import lax
from jax.experimental import pallas as pl
from jax.experimental.pallas import tpu as pltpu

F32 = jnp.float32
BF16 = jnp.bfloat16

LANES = 128
SUBLANES = 8

D_MODEL = 1024
DEPTH = 4
PAGE_SIZE = 128
POOL_WIDTH = D_MODEL // 2
POOL_WINDOWS = (2, 4, 8, 16)
POOL_GROUP = POOL_WIDTH // len(POOL_WINDOWS)
POOL_STATE = max(POOL_WINDOWS) - 1
POOL_HALO = 16
N_HEADS = 8
HEAD_DIM = 64
ATTN_WIDTH = N_HEADS * HEAD_DIM
N_IDX_HEADS = 8
IDX_DIM = 64
TOPK_MAX = 256
N_BUCKETS = 32
MAX_DISTANCE = 128
D_FF = 2816
CONV_WIDTH = 3
EPS = 1e-6

COL_U, COL_Q, COL_K, COL_V, COL_QI, COL_KW, COL_KI2, COL_END = 0, 512, 1024, 1536, 2048, 2560, 2688, 2816
IN_WIDTH = 2632

NEG_BIG = -1e30
BISECT_ITERS = 40
TINY = 1e-30

MIB = 1024 * 1024


def _params(sem, vmem_mib):
    return pltpu.CompilerParams(dimension_semantics=sem, vmem_limit_bytes=vmem_mib * MIB)


def _rows(ref, tm):
    v = ref[...]
    return v if v.shape[0] == tm else v[0:1]


def _norm_mod(x, g, sc, sh):
    y = x * lax.rsqrt(jnp.mean(x * x, axis=-1, keepdims=True) + EPS)
    return (y * g) * (1.0 + sc) + sh


def _silu(x):
    return x * (1.0 / (1.0 + jnp.exp(-x)))


def _dot_t(a, b):
    return lax.dot_general(a, b, (((1,), (1,)), ((), ())), preferred_element_type=F32)


def _mod_kernel(c_ref, w_ref, b_ref, o_ref):
    a = _silu(c_ref[...]).astype(BF16)
    o_ref[0] = jnp.dot(a, w_ref[0], preferred_element_type=F32) + b_ref[0]


def _modulation(c_all, w_mod_b, b_mod):
    n, d = c_all.shape
    depth, _, width = w_mod_b.shape
    tn = 1536
    return pl.pallas_call(
        _mod_kernel,
        grid=(depth, width // tn),
        in_specs=[pl.BlockSpec((n, d), lambda l, j: (0, 0)),
                  pl.BlockSpec((1, d, tn), lambda l, j: (l, 0, j)),
                  pl.BlockSpec((1, 1, tn), lambda l, j: (l, 0, j))],
        out_specs=pl.BlockSpec((1, n, tn), lambda l, j: (l, 0, j)),
        out_shape=jax.ShapeDtypeStruct((depth, n, width), F32),
        compiler_params=_params(("arbitrary", "arbitrary"), 32),
        name="modulation",
    )(c_all, w_mod_b, b_mod.reshape(depth, 1, width))


def _bias_table_kernel(rb_ref, bk_ref, o_ref):
    h = pl.program_id(0)
    bk = bk_ref[...]
    acc = jnp.zeros(bk.shape, F32)
    for b in range(N_BUCKETS):
        acc = jnp.where(bk == b, rb_ref[b * N_HEADS + h], acc)
    o_ref[0] = acc - rb_ref[(N_BUCKETS - 1) * N_HEADS + h]


def _bias_table(rel_bias, buckets):
    r, n = buckets.shape
    return pl.pallas_call(
        _bias_table_kernel,
        grid=(N_HEADS,),
        in_specs=[pl.BlockSpec(memory_space=pltpu.SMEM),
                  pl.BlockSpec((r, n), lambda h: (0, 0))],
        out_specs=pl.BlockSpec((1, r, n), lambda h: (h, 0, 0)),
        out_shape=jax.ShapeDtypeStruct((N_HEADS, r, n), F32),
        compiler_params=_params(("arbitrary",), 32),
        name="bias_table",
    )(rel_bias.reshape(-1), buckets)


def _t5_bucket_np(n):
    max_exact = N_BUCKETS // 2
    nf = np.maximum(n, 1).astype(np.float64)
    large = max_exact + (np.log(nf / max_exact) / np.log(MAX_DISTANCE / max_exact) * (N_BUCKETS - max_exact)).astype(np.int64)
    large = np.minimum(large, N_BUCKETS - 1)
    return np.where(n < max_exact, n, large).astype(np.int32)


def _inproj_kernel(x_ref, g_ref, sc_ref, sh_ref, w_ref,
                   u_ref, q_ref, k_ref, v_ref, kb_ref, vb_ref, qi_ref, kw_ref, ki2_ref, *, tm):
    h = _norm_mod(x_ref[...], g_ref[...], _rows(sc_ref, tm), _rows(sh_ref, tm)).astype(BF16)

    def mm(c0, c1):
        return jnp.dot(h, w_ref[:, c0:c1], preferred_element_type=F32)

    u_ref[...] = mm(COL_U, COL_Q)
    q_ref[...] = (mm(COL_Q, COL_K) * (HEAD_DIM ** -0.5)).astype(BF16)
    k = mm(COL_K, COL_V)
    k_ref[...] = k
    kb_ref[...] = k.astype(BF16)
    v = mm(COL_V, COL_QI)
    v_ref[...] = v
    vb_ref[...] = v.astype(BF16)
    qi_ref[...] = (mm(COL_QI, COL_KW) * (IDX_DIM ** -0.5)).astype(BF16)
    kw = mm(COL_KW, COL_KI2)
    lane = lax.broadcasted_iota(jnp.int32, kw.shape, 1)
    kw_ref[...] = jnp.where(lane >= IDX_DIM, kw * (N_IDX_HEADS ** -0.5), kw)
    ki2_ref[...] = mm(COL_KI2, COL_END).astype(BF16)


def _inproj(x, g, sc, sh, w_aug, *, tm, mod_rows):
    n, d = x.shape
    mod_spec = _mod_spec(n, tm, d, mod_rows, sc)
    row = lambda w: pl.BlockSpec((tm, w), lambda i: (i, 0))
    outs = [(POOL_WIDTH, F32), (ATTN_WIDTH, BF16), (ATTN_WIDTH, F32), (ATTN_WIDTH, F32), (ATTN_WIDTH, BF16),
            (ATTN_WIDTH, BF16), (N_IDX_HEADS * IDX_DIM, BF16), (LANES, F32), (LANES, BF16)]
    return pl.pallas_call(
        functools.partial(_inproj_kernel, tm=tm),
        grid=(n // tm,),
        in_specs=[row(d), pl.BlockSpec((1, d), lambda i: (0, 0)), mod_spec, mod_spec,
                  pl.BlockSpec((d, COL_END), lambda i: (0, 0))],
        out_specs=[row(w) for w, _ in outs],
        out_shape=[jax.ShapeDtypeStruct((n, w), t) for w, t in outs],
        compiler_params=_params(("arbitrary",), 48),
        name="inproj",
    )(x, g, sc, sh, w_aug)


def _pool_group(s, cnt, u_g, w_g, ps_g):
    d = s / cnt - u_g
    return (jnp.dot(d.astype(BF16), w_g, preferred_element_type=F32) * ps_g).astype(BF16)


def _pool_prompt_kernel(u_ref, halo_ref, w_ref, ps_ref, o_ref, ext_ref, *, tp, tiles_per_seq):
    i = pl.program_id(0) % tiles_per_seq
    ext_ref[0:POOL_HALO, :] = jnp.where(i == 0, 0.0, halo_ref[...])
    ext_ref[POOL_HALO:, :] = u_ref[...]
    pos = i * tp + lax.broadcasted_iota(jnp.int32, (tp, POOL_GROUP), 0)
    for g, w in enumerate(POOL_WINDOWS):
        sl = slice(g * POOL_GROUP, (g + 1) * POOL_GROUP)
        s = ext_ref[POOL_HALO:POOL_HALO + tp, sl]
        for j in range(1, w):
            s = s + ext_ref[POOL_HALO - j:POOL_HALO - j + tp, sl]
        cnt = jnp.minimum(w, pos + 1).astype(F32)
        o_ref[:, sl] = _pool_group(s, cnt, u_ref[:, sl], w_ref[g], ps_ref[:, sl])


def _pool_prompt(u, w_pool_b, pool_scale, *, tp, seq):
    n, p = u.shape
    tiles_per_seq = seq // tp
    hb = tp // POOL_HALO
    return pl.pallas_call(
        functools.partial(_pool_prompt_kernel, tp=tp, tiles_per_seq=tiles_per_seq),
        grid=(n // tp,),
        in_specs=[pl.BlockSpec((tp, p), lambda i: (i, 0)),
                  pl.BlockSpec((POOL_HALO, p), lambda i: (jnp.maximum(i * hb - 1, 0), 0)),
                  pl.BlockSpec(w_pool_b.shape, lambda i: (0, 0, 0)),
                  pl.BlockSpec((1, p), lambda i: (0, 0))],
        out_specs=pl.BlockSpec((tp, p), lambda i: (i, 0)),
        out_shape=jax.ShapeDtypeStruct((n, p), BF16),
        scratch_shapes=[pltpu.VMEM((tp + POOL_HALO, p), F32)],
        compiler_params=_params(("arbitrary",), 32),
        name="pool_prompt",
    )(u, u, w_pool_b, pool_scale.reshape(1, p))


def _pool_sample_kernel(ext_ref, w_ref, ps_ref, o_ref, *, t_new, pos0):
    for t in range(t_new):
        for g, w in enumerate(POOL_WINDOWS):
            sl = slice(g * POOL_GROUP, (g + 1) * POOL_GROUP)
            s = ext_ref[POOL_STATE + t, :, sl]
            for j in range(1, w):
                s = s + ext_ref[POOL_STATE + t - j, :, sl]
            cnt = float(min(w, pos0 + t + 1))
            o_ref[t, :, sl] = _pool_group(s, cnt, ext_ref[POOL_STATE + t, :, sl], w_ref[g], ps_ref[:, sl])


def _pool_sample(ext_tm, w_pool_b, pool_scale, *, t_new, pos0):
    _, nb, p = ext_tm.shape
    return pl.pallas_call(
        functools.partial(_pool_sample_kernel, t_new=t_new, pos0=pos0),
        out_shape=jax.ShapeDtypeStruct((t_new, nb, p), BF16),
        compiler_params=pltpu.CompilerParams(vmem_limit_bytes=32 * MIB),
        name="pool_sample",
    )(ext_tm, w_pool_b, pool_scale.reshape(1, p))


def _attn_prompt_kernel(q_ref, qi_ref, kw_ref, ki2_ref, k_ref, v_ref, bias_ref, o_ref,
                        sc_ref, qm_ref, qim_ref, wib_ref, m_ref, l_ref, acc_ref, *, tq, topk, idx_bits):
    i = pl.program_id(1)
    nk = i + 1
    rep = tq // LANES
    wide = lambda a: jnp.concatenate([a] * rep, axis=1)
    low_half = lax.broadcasted_iota(jnp.int32, (tq, LANES), 1) < HEAD_DIM

    kw = kw_ref[0]
    for hp in range(N_HEADS // 2):
        sl = slice(hp * LANES, (hp + 1) * LANES)
        for ref, src in ((qm_ref, q_ref), (qim_ref, qi_ref)):
            x = src[0, :, sl]
            zero = jnp.zeros_like(x)
            ref[2 * hp] = jnp.where(low_half, x, zero)
            ref[2 * hp + 1] = jnp.where(low_half, zero, x)
    for h in range(N_IDX_HEADS):
        wib_ref[h] = jnp.broadcast_to(kw[:, IDX_DIM + h:IDX_DIM + h + 1], (tq, LANES))

    row = lax.broadcasted_iota(jnp.int32, (tq, tq), 0)
    col = lax.broadcasted_iota(jnp.int32, (tq, tq), 1)

    def score_tile(t, diag):
        kt = ki2_ref[0, pl.ds(pl.multiple_of(t * tq, tq), tq), :]
        s_acc = jnp.zeros((tq, tq), F32)
        for h in range(N_IDX_HEADS):
            s = _dot_t(qim_ref[h], kt)
            s_acc = s_acc + wide(wib_ref[h]) * jnp.maximum(s, 0.0)
        if diag:
            s_acc = jnp.where(col <= row, s_acc, -jnp.inf)
        sc_ref[t] = s_acc

    def score_body(t, c):
        score_tile(t, False)
        return c

    lax.fori_loop(0, i, score_body, 0)
    score_tile(i, True)

    def reduce_tiles(fn, init):
        def body(t, acc):
            blk = sc_ref[t]
            for u in range(rep):
                acc = fn(acc, blk[:, u * LANES:(u + 1) * LANES], t * tq + u * LANES)
            return acc
        return lax.fori_loop(0, nk, body, init)

    def count_ge(thr):
        thr_b = jnp.broadcast_to(thr, (tq, LANES))
        acc = reduce_tiles(lambda a, x, c0: a + jnp.where(x >= thr_b, 1.0, 0.0), jnp.zeros((tq, LANES), F32))
        return jnp.sum(acc, axis=1, keepdims=True)

    qpos = i * tq + lax.broadcasted_iota(jnp.int32, (tq, 1), 0)
    short = qpos < topk
    fk = float(topk)

    mx = reduce_tiles(lambda a, x, c0: jnp.maximum(a, x), jnp.full((tq, LANES), -jnp.inf, F32))
    mn = reduce_tiles(lambda a, x, c0: jnp.minimum(a, jnp.where(x == -jnp.inf, jnp.inf, x)),
                      jnp.full((tq, LANES), jnp.inf, F32))
    mx = jnp.max(mx, axis=1, keepdims=True)
    mn = jnp.min(mn, axis=1, keepdims=True)
    lo0 = mn
    hi0 = mx + (jnp.abs(mx) * 2.0 ** -20 + TINY)
    clo0 = jnp.where(short, fk, (qpos + 1).astype(F32))

    def unresolved(clo):
        return jnp.max(jnp.where(jnp.logical_and(clo != fk, jnp.logical_not(short)), 1, 0))

    def bis_cond(carry):
        it, _, _, _, pending = carry
        return jnp.logical_and(it < BISECT_ITERS, pending > 0)

    def bis_body(carry):
        it, lo, hi, clo, _ = carry
        mid = 0.5 * (lo + hi)
        c = count_ge(mid)
        ge = c >= fk
        lo = jnp.where(ge, mid, lo)
        clo = jnp.where(ge, c, clo)
        hi = jnp.where(ge, hi, mid)
        return it + 1, lo, hi, clo, unresolved(clo)

    _, lo, hi, clo, pending = lax.while_loop(bis_cond, bis_body, (jnp.int32(0), lo0, hi0, clo0, unresolved(clo0)))

    @pl.when(pending > 0)
    def _resolve_ties():
        tie_row = jnp.logical_and(clo != fk, jnp.logical_not(short))
        need = fk - count_ge(hi)
        lo_b = jnp.broadcast_to(lo, (tq, LANES))
        hi_b = jnp.broadcast_to(hi, (tq, LANES))
        lane = lax.broadcasted_iota(jnp.int32, (tq, LANES), 1)

        def in_group(x):
            return jnp.logical_and(x >= lo_b, x < hi_b)

        def count_group_below(bound):
            bound_b = jnp.broadcast_to(bound, (tq, LANES))
            acc = reduce_tiles(
                lambda a, x, c0: a + jnp.where(jnp.logical_and(in_group(x), lane + c0 < bound_b), 1.0, 0.0),
                jnp.zeros((tq, LANES), F32))
            return jnp.sum(acc, axis=1, keepdims=True)

        def idx_body(s, jb):
            cand = jb + jnp.left_shift(1, idx_bits - 1 - s)
            return jnp.where(count_group_below(cand) < need, cand, jb)

        jb = lax.fori_loop(0, idx_bits, idx_body, jnp.zeros((tq, 1), jnp.int32))
        jb = jnp.where(tie_row, jb, jnp.int32(2 ** 30))
        jb_b = jnp.broadcast_to(jb, (tq, LANES))

        def drop_body(t, c):
            blk = sc_ref[t]
            parts = []
            for u in range(rep):
                x = blk[:, u * LANES:(u + 1) * LANES]
                drop = jnp.logical_and(in_group(x), lane + (t * tq + u * LANES) > jb_b)
                parts.append(jnp.where(drop, -jnp.inf, x))
            sc_ref[t] = jnp.concatenate(parts, axis=1)
            return c

        lax.fori_loop(0, nk, drop_body, 0)

    lo = jnp.where(short, NEG_BIG, lo)
    lo_w = wide(jnp.broadcast_to(lo, (tq, LANES)))

    m_ref[...] = jnp.full(m_ref.shape, NEG_BIG, F32)
    l_ref[...] = jnp.zeros(l_ref.shape, F32)
    acc_ref[...] = jnp.zeros(acc_ref.shape, F32)

    def attend(t, bias_sel):
        start = pl.multiple_of(t * tq, tq)
        kt = k_ref[0, pl.ds(start, tq), :]
        vt = v_ref[0, pl.ds(start, tq), :]
        mask = sc_ref[t] >= lo_w
        for h in range(N_HEADS):
            sl = slice((h // 2) * LANES, (h // 2 + 1) * LANES)
            lg = _dot_t(qm_ref[h], kt[:, sl])
            if bias_sel is not None:
                lg = lg + bias_ref[h, :, bias_sel * tq:(bias_sel + 1) * tq]
            lg = jnp.where(mask, lg, NEG_BIG)
            m_old = m_ref[h]
            m_new = jnp.maximum(m_old, jnp.broadcast_to(jnp.max(lg, axis=1, keepdims=True), (tq, LANES)))
            p = jnp.exp(lg - wide(m_new))
            alpha = jnp.exp(m_old - m_new)
            l_ref[h] = alpha * l_ref[h] + jnp.broadcast_to(jnp.sum(p, axis=1, keepdims=True), (tq, LANES))
            acc_ref[h] = alpha * acc_ref[h] + jnp.dot(p.astype(BF16), vt[:, sl], preferred_element_type=F32)
            m_ref[h] = m_new

    def attend_body(t, c):
        attend(t, None)
        return c

    lax.fori_loop(0, jnp.maximum(i - 1, 0), attend_body, 0)

    @pl.when(i >= 1)
    def _near():
        attend(i - 1, 1)

    attend(i, 0)

    for hp in range(N_HEADS // 2):
        even = acc_ref[2 * hp] / l_ref[2 * hp]
        odd = acc_ref[2 * hp + 1] / l_ref[2 * hp + 1]
        o_ref[0, :, hp * LANES:(hp + 1) * LANES] = jnp.where(low_half, even, odd).astype(BF16)


def _attn_prompt(q, qi, kw, ki2, kb, vb, bias, *, tq, topk):
    b, t, _ = q.shape
    assert t % tq == 0 and tq % LANES == 0 and tq >= topk
    nt = t // tq
    tile = lambda w: pl.BlockSpec((1, tq, w), lambda bi, i: (bi, i, 0))
    whole = lambda w: pl.BlockSpec((1, t, w), lambda bi, i: (bi, 0, 0), pipeline_mode=pl.Buffered(1))
    return pl.pallas_call(
        functools.partial(_attn_prompt_kernel, tq=tq, topk=topk, idx_bits=max(1, int(np.ceil(np.log2(t))))),
        grid=(b, nt),
        in_specs=[tile(ATTN_WIDTH), tile(N_IDX_HEADS * IDX_DIM), tile(LANES), whole(LANES),
                  whole(ATTN_WIDTH), whole(ATTN_WIDTH),
                  pl.BlockSpec(bias.shape, lambda bi, i: (0, 0, 0), pipeline_mode=pl.Buffered(1))],
        out_specs=tile(ATTN_WIDTH),
        out_shape=jax.ShapeDtypeStruct((b, t, ATTN_WIDTH), BF16),
        scratch_shapes=[pltpu.VMEM((nt, tq, tq), F32),
                        pltpu.VMEM((N_HEADS, tq, LANES), BF16),
                        pltpu.VMEM((N_IDX_HEADS, tq, LANES), BF16),
                        pltpu.VMEM((N_IDX_HEADS, tq, LANES), F32),
                        pltpu.VMEM((N_HEADS, tq, LANES), F32),
                        pltpu.VMEM((N_HEADS, tq, LANES), F32),
                        pltpu.VMEM((N_HEADS, tq, LANES), F32)],
        compiler_params=_params(("arbitrary", "arbitrary"), 56),
        name="attn_prompt",
    )(q, qi, kw, ki2, kb, vb, bias)


def _attn_sample_kernel(pt_ref, qi_ref, wi_ref, q_ref, kin_ref, kn_ref, vn_ref, bias_ref,
                        cidx_hbm, ck_hbm, cv_hbm, o_ref,
                        sidx, sk, sv, kie, ke, ve, sems, *, layer, n_pages, t_new, topk, idx_bits):
    b = pl.program_id(0)
    nb = pl.num_programs(0)
    slot = b % 2
    past = n_pages * PAGE_SIZE
    pad = kie.shape[0] - past
    n_keys = past + pad

    def page_copies(bb, sl):
        out = []
        for p in range(n_pages):
            pg = pt_ref[bb, p]
            out.append(pltpu.make_async_copy(cidx_hbm.at[layer, pg], sidx.at[sl, p], sems.at[0, sl]))
            out.append(pltpu.make_async_copy(ck_hbm.at[layer, pg], sk.at[sl, p], sems.at[1, sl]))
            out.append(pltpu.make_async_copy(cv_hbm.at[layer, pg], sv.at[sl, p], sems.at[2, sl]))
        return out

    @pl.when(b == 0)
    def _first():
        for c in page_copies(0, 0):
            c.start()

    @pl.when(b + 1 < nb)
    def _prefetch():
        for c in page_copies(b + 1, 1 - slot):
            c.start()

    for c in page_copies(b, slot):
        c.wait()

    for p in range(n_pages):
        rows = slice(p * PAGE_SIZE, (p + 1) * PAGE_SIZE)
        kie[rows, 0:IDX_DIM] = sidx[slot, p].astype(BF16)
        ke[rows, :] = sk[slot, p].astype(BF16)
        ve[rows, :] = sv[slot, p].astype(BF16)
    kie[:, IDX_DIM:] = jnp.zeros((n_keys, LANES - IDX_DIM), BF16)
    new = kn_ref.shape[1]
    kie[past:past + new, 0:IDX_DIM] = kin_ref[0][:, 0:IDX_DIM].astype(BF16)
    ke[past:past + new, :] = kn_ref[0].astype(BF16)
    ve[past:past + new, :] = vn_ref[0].astype(BF16)
    if pad > new:
        kie[past + new:, 0:IDX_DIM] = jnp.zeros((pad - new, IDX_DIM), BF16)
        ke[past + new:, :] = jnp.zeros((pad - new, ATTN_WIDTH), BF16)
        ve[past + new:, :] = jnp.zeros((pad - new, ATTN_WIDTH), BF16)

    s = _dot_t(qi_ref[0], kie[...])
    wi = wi_ref[0]
    score = jnp.zeros((SUBLANES, n_keys), F32)
    for h in range(N_IDX_HEADS):
        rs = slice(h * SUBLANES, (h + 1) * SUBLANES)
        score = score + wi[rs] * jnp.maximum(s[rs], 0.0)
    row = lax.broadcasted_iota(jnp.int32, (SUBLANES, n_keys), 0)
    col = lax.broadcasted_iota(jnp.int32, (SUBLANES, n_keys), 1)
    adm = jnp.logical_or(col < past, jnp.logical_and(col - past <= row, col < past + t_new))
    score = jnp.where(adm, score, -jnp.inf)

    fk = float(topk)

    def count_ge(thr):
        return jnp.sum(jnp.where(score >= thr, 1.0, 0.0), axis=1, keepdims=True)

    mx = jnp.max(score, axis=1, keepdims=True)
    mn = jnp.min(jnp.where(adm, score, jnp.inf), axis=1, keepdims=True)
    lo0 = mn
    hi0 = mx + (jnp.abs(mx) * 2.0 ** -20 + TINY)

    def bis_body(_, carry):
        lo, hi = carry
        mid = 0.5 * (lo + hi)
        ge = count_ge(mid) >= fk
        return jnp.where(ge, mid, lo), jnp.where(ge, hi, mid)

    lo, hi = lax.fori_loop(0, BISECT_ITERS, bis_body, (lo0, hi0))
    need = fk - count_ge(hi)
    group = jnp.logical_and(score >= lo, score < hi)

    def idx_body(st, jb):
        cand = jb + jnp.left_shift(1, idx_bits - 1 - st)
        below = jnp.sum(jnp.where(jnp.logical_and(group, col < cand), 1.0, 0.0), axis=1, keepdims=True)
        return jnp.where(below < need, cand, jb)

    jb = lax.fori_loop(0, idx_bits, idx_body, jnp.zeros((SUBLANES, 1), jnp.int32))
    sel = jnp.logical_or(score >= hi, jnp.logical_and(group, col <= jb))

    hrow = lax.broadcasted_iota(jnp.int32, (N_HEADS * SUBLANES, ATTN_WIDTH), 0) // SUBLANES
    hlane = lax.broadcasted_iota(jnp.int32, (N_HEADS * SUBLANES, ATTN_WIDTH), 1) // HEAD_DIM
    own = hrow == hlane
    q8 = q_ref[0]
    qbd = jnp.where(own, jnp.concatenate([q8] * N_HEADS, axis=0), 0.0).astype(BF16)
    lg = _dot_t(qbd, ke[...]) + bias_ref[...]
    mask = jnp.concatenate([sel] * N_HEADS, axis=0)
    lg = jnp.where(mask, lg, NEG_BIG)
    m = jnp.max(lg, axis=1, keepdims=True)
    p = jnp.exp(lg - m)
    den = jnp.sum(p, axis=1, keepdims=True)
    o = jnp.dot(p.astype(BF16), ve[...], preferred_element_type=F32) / den
    o = jnp.where(own, o, 0.0)
    out = o[0:SUBLANES]
    for h in range(1, N_HEADS):
        out = out + o[h * SUBLANES:(h + 1) * SUBLANES]
    o_ref[0] = out


def _attn_sample(page_table, qi_bd, wi_col, q8, kin, kn, vn, bias, cache_idx_k, cache_k, cache_v, *, layer, t_new, topk):
    nb, n_pages = page_table.shape
    past = n_pages * PAGE_SIZE
    assert past + 1 > topk
    new = kn.shape[1]
    n_keys = past + LANES
    per_b = lambda shape: pl.BlockSpec((1,) + shape, lambda b, pt: (b, 0, 0))
    grid_spec = pltpu.PrefetchScalarGridSpec(
        num_scalar_prefetch=1,
        grid=(nb,),
        in_specs=[per_b((N_HEADS * SUBLANES, LANES)), per_b((N_HEADS * SUBLANES, 1)), per_b((SUBLANES, ATTN_WIDTH)),
                  per_b((new, LANES)), per_b((new, ATTN_WIDTH)), per_b((new, ATTN_WIDTH)),
                  pl.BlockSpec(bias.shape, lambda b, pt: (0, 0)),
                  pl.BlockSpec(memory_space=pl.ANY), pl.BlockSpec(memory_space=pl.ANY),
                  pl.BlockSpec(memory_space=pl.ANY)],
        out_specs=per_b((SUBLANES, ATTN_WIDTH)),
        scratch_shapes=[pltpu.VMEM((2, n_pages, PAGE_SIZE, IDX_DIM), F32),
                        pltpu.VMEM((2, n_pages, PAGE_SIZE, ATTN_WIDTH), F32),
                        pltpu.VMEM((2, n_pages, PAGE_SIZE, ATTN_WIDTH), F32),
                        pltpu.VMEM((n_keys, LANES), BF16),
                        pltpu.VMEM((n_keys, ATTN_WIDTH), BF16),
                        pltpu.VMEM((n_keys, ATTN_WIDTH), BF16),
                        pltpu.SemaphoreType.DMA((3, 2))],
    )
    return pl.pallas_call(
        functools.partial(_attn_sample_kernel, layer=layer, n_pages=n_pages, t_new=t_new, topk=topk,
                          idx_bits=max(1, int(np.ceil(np.log2(n_keys))))),
        grid_spec=grid_spec,
        out_shape=jax.ShapeDtypeStruct((nb, SUBLANES, ATTN_WIDTH), F32),
        compiler_params=_params(("arbitrary",), 48),
        name="attn_sample",
    )(page_table, qi_bd, wi_col, q8, kin, kn, vn, bias, cache_idx_k, cache_k, cache_v)


def _outproj_kernel(x_ref, p_ref, a_ref, g_ref, w_ref, o_ref, *, tm):
    mixed = (jnp.dot(p_ref[...], w_ref[0:POOL_WIDTH, :], preferred_element_type=F32)
             + jnp.dot(a_ref[...], w_ref[POOL_WIDTH:, :], preferred_element_type=F32))
    o_ref[...] = x_ref[...] + _rows(g_ref, tm) * mixed


def _mod_spec(n, tm, d, mod_rows, arr):
    if mod_rows == tm:
        return pl.BlockSpec((tm, d), lambda i, *_: (i, 0))
    per = (n // tm) // (arr.shape[0] // SUBLANES)
    return pl.BlockSpec((SUBLANES, d), lambda i, *_: (i // per, 0))


def _outproj(x, pool, attn, gate, w_out_b, *, tm, mod_rows):
    n, d = x.shape
    row = lambda w: pl.BlockSpec((tm, w), lambda i: (i, 0))
    return pl.pallas_call(
        functools.partial(_outproj_kernel, tm=tm),
        grid=(n // tm,),
        in_specs=[row(d), row(POOL_WIDTH), row(ATTN_WIDTH), _mod_spec(n, tm, d, mod_rows, gate),
                  pl.BlockSpec(w_out_b.shape, lambda i: (0, 0))],
        out_specs=row(d),
        out_shape=jax.ShapeDtypeStruct((n, d), F32),
        compiler_params=_params(("arbitrary",), 32),
        name="outproj",
    )(x, pool, attn, gate, w_out_b)


def _ffn_kernel(*refs, tm, tf, shift, halo, tiles_per_seq, from_state):
    if from_state:
        (x_ref, gn_ref, sc_ref, sh_ref, gate_ref, wa_ref, wg_ref, ca_ref, cg_ref, wd_ref, sa_ref, sg_ref,
         o_ref, ta_ref, tg_ref, h_ref, acc_ref, ea_ref, eg_ref) = refs
    else:
        (x_ref, gn_ref, sc_ref, sh_ref, gate_ref, wa_ref, wg_ref, ca_ref, cg_ref, wd_ref,
         o_ref, ta_ref, tg_ref, h_ref, acc_ref, ea_ref, eg_ref, cra_ref, crg_ref) = refs
    i = pl.program_id(0)
    j = pl.program_id(1)

    @pl.when(j == 0)
    def _start():
        h_ref[...] = _norm_mod(x_ref[...], gn_ref[...], _rows(sc_ref, tm), _rows(sh_ref, tm)).astype(BF16)
        acc_ref[...] = jnp.zeros(acc_ref.shape, F32)

    h = h_ref[...]

    def conv_half(w_ref, c_ref, e_ref, state_ref, carry_ref, tail_ref):
        up = jnp.dot(h, w_ref[...], preferred_element_type=F32)
        if from_state:
            e_ref[0:halo, :] = state_ref[...]
        else:
            e_ref[0:halo, :] = jnp.where(i % tiles_per_seq == 0, 0.0, carry_ref[j])
        e_ref[halo:, :] = up
        y = c_ref[3:4, :] + c_ref[0:1, :] * e_ref[halo - 2 * shift:halo - 2 * shift + tm, :]
        y = y + c_ref[1:2, :] * e_ref[halo - shift:halo - shift + tm, :]
        y = y + c_ref[2:3, :] * up
        tail = e_ref[tm:tm + halo, :]
        if not from_state:
            carry_ref[j] = tail
        tail_ref[0] = tail
        return y

    if from_state:
        a = conv_half(wa_ref, ca_ref, ea_ref, sa_ref, None, ta_ref)
        g = conv_half(wg_ref, cg_ref, eg_ref, sg_ref, None, tg_ref)
    else:
        a = conv_half(wa_ref, ca_ref, ea_ref, None, cra_ref, ta_ref)
        g = conv_half(wg_ref, cg_ref, eg_ref, None, crg_ref, tg_ref)
    act = (_silu(g) * a).astype(BF16)
    acc_ref[...] += jnp.dot(act, wd_ref[...], preferred_element_type=F32)

    @pl.when(j == pl.num_programs(1) - 1)
    def _finish():
        o_ref[...] = x_ref[...] + _rows(gate_ref, tm) * acc_ref[...]


def _ffn(x, gn, sc, sh, gate, w_up_b, conv_pack, w_down_b, state, *, tm, tf, shift, halo, seq_rows, mod_rows):
    n, d = x.shape
    nff = D_FF // tf
    ni = n // tm
    from_state = state is not None
    assert (not from_state) or ni == 1
    row = pl.BlockSpec((tm, d), lambda i, j: (i, 0))
    mod = lambda arr: _mod_spec(n, tm, d, mod_rows, arr)
    in_specs = [row, pl.BlockSpec((1, d), lambda i, j: (0, 0)), mod(sc), mod(sh), mod(gate),
                pl.BlockSpec((d, tf), lambda i, j: (0, j)), pl.BlockSpec((d, tf), lambda i, j: (0, j + nff)),
                pl.BlockSpec((SUBLANES, tf), lambda i, j: (0, j)), pl.BlockSpec((SUBLANES, tf), lambda i, j: (0, j + nff)),
                pl.BlockSpec((tf, d), lambda i, j: (j, 0))]
    args = [x, gn, sc, sh, gate, w_up_b, w_up_b, conv_pack, conv_pack, w_down_b]
    scratch = [pltpu.VMEM((tm, d), BF16), pltpu.VMEM((tm, d), F32),
               pltpu.VMEM((tm + halo, tf), F32), pltpu.VMEM((tm + halo, tf), F32)]
    if from_state:
        in_specs += [pl.BlockSpec((halo, tf), lambda i, j: (0, j)), pl.BlockSpec((halo, tf), lambda i, j: (0, j + nff))]
        args += [state, state]
    else:
        scratch += [pltpu.VMEM((nff, halo, tf), F32), pltpu.VMEM((nff, halo, tf), F32)]
    tail_spec = pl.BlockSpec((1, halo, tf), lambda i, j: (i, 0, j))
    tail_shape = jax.ShapeDtypeStruct((ni, halo, D_FF), F32)
    return pl.pallas_call(
        functools.partial(_ffn_kernel, tm=tm, tf=tf, shift=shift, halo=halo,
                          tiles_per_seq=seq_rows // tm, from_state=from_state),
        grid=(ni, nff),
        in_specs=in_specs,
        out_specs=[row, tail_spec, tail_spec],
        out_shape=[jax.ShapeDtypeStruct((n, d), F32), tail_shape, tail_shape],
        scratch_shapes=scratch,
        compiler_params=_params(("arbitrary", "arbitrary"), 56),
        name="conv_ffn",
    )(*args)


def _final_norm_kernel(x_ref, g_ref, o_ref):
    x = x_ref[...]
    o_ref[...] = (x * lax.rsqrt(jnp.mean(x * x, axis=-1, keepdims=True) + EPS)) * g_ref[...]


def _final_norm(x, g, *, tm):
    n, d = x.shape
    return pl.pallas_call(
        _final_norm_kernel,
        grid=(n // tm,),
        in_specs=[pl.BlockSpec((tm, d), lambda i: (i, 0)), pl.BlockSpec((1, d), lambda i: (0, 0))],
        out_specs=pl.BlockSpec((tm, d), lambda i: (i, 0)),
        out_shape=jax.ShapeDtypeStruct((n, d), F32),
        compiler_params=_params(("arbitrary",), 32),
        name="final_norm",
    )(x, g.reshape(1, d))


def _prompt_bias_buckets(tq):
    r = np.arange(tq)[:, None]
    c = np.arange(tq)[None, :]
    diag = _t5_bucket_np(np.maximum(r - c, 0))
    prev = _t5_bucket_np(np.maximum(tq + r - c, 0))
    return np.concatenate([diag, prev], axis=1)


def _sample_bias_buckets(past, t_new, n_keys):
    t = np.minimum(np.arange(SUBLANES), t_new - 1)[:, None]
    j = np.arange(n_keys)[None, :]
    return _t5_bucket_np(np.maximum(past + t - j, 0))


def kernel(x_prompt, x_sample, c_prompt, c_sample, cache_k, cache_v, cache_idx_k, state_pool, state_conv,
           page_table, w_mod, b_mod, norm_attn_g, norm_ffn_g, w_in, w_pool, pool_scale, rel_bias, w_out,
           w_up, conv_w, conv_b, w_down, final_norm_g):
    bp, seq, d = x_prompt.shape
    nb, t_new, _ = x_sample.shape
    depth = w_in.shape[0]
    n_pages = page_table.shape[1]
    past = n_pages * PAGE_SIZE
    n_pool = cache_k.shape[1]
    ns = nb * t_new
    tq = 256
    tm_p, tm_s = 512, ns
    topk_p = min(TOPK_MAX, seq // 4)
    topk_s = min(TOPK_MAX, (past + t_new) // 4)

    w_in_b = w_in.astype(BF16)
    w_ki = w_in_b[:, :, COL_KW:COL_KW + IDX_DIM]
    w_aug = jnp.concatenate([w_in_b, jnp.zeros((depth, d, COL_KI2 - IN_WIDTH), BF16), w_ki, w_ki], axis=2)
    w_mod_b = w_mod.astype(BF16)
    w_pool_b = w_pool.astype(BF16)
    w_out_b = w_out.astype(BF16)
    w_up_b = w_up.astype(BF16)
    w_down_b = w_down.astype(BF16)
    conv_pack = jnp.concatenate([conv_w, conv_b[:, None, :], jnp.zeros((depth, SUBLANES - CONV_WIDTH - 1, 2 * D_FF), F32)], axis=1)

    n_c = bp + nb
    c_all = jnp.concatenate([c_prompt, c_sample, jnp.zeros((-n_c % SUBLANES, d), F32)], axis=0)
    mod = _modulation(c_all, w_mod_b, b_mod)
    mod = mod.reshape(depth, mod.shape[1], 6, d)

    def mod_p(l, k):
        return jnp.broadcast_to(mod[l, :bp, k][:, None, :], (bp, SUBLANES, d)).reshape(bp * SUBLANES, d)

    def mod_s(l, k):
        return jnp.tile(mod[l, bp:bp + nb, k], (t_new, 1))

    bias_p = _bias_table(rel_bias, jnp.asarray(_prompt_bias_buckets(tq)))
    n_keys_s = past + LANES
    bias_s = _bias_table(rel_bias, jnp.asarray(_sample_bias_buckets(past, t_new, n_keys_s)))
    bias_s = bias_s.reshape(N_HEADS * SUBLANES, n_keys_s)

    cache_k2 = cache_k.reshape(depth, n_pool, PAGE_SIZE, ATTN_WIDTH)
    cache_v2 = cache_v.reshape(depth, n_pool, PAGE_SIZE, ATTN_WIDTH)

    xp = x_prompt.reshape(bp * seq, d)
    xs = jnp.swapaxes(x_sample, 0, 1).reshape(ns, d)

    def to_bt(a):
        return jnp.swapaxes(a.reshape(t_new, nb, a.shape[-1]), 0, 1)

    def pad_t(a, rows):
        return jnp.pad(a, ((0, 0), (0, rows - a.shape[1]), (0, 0)))

    outs_p = [[] for _ in range(5)]
    outs_s = [[] for _ in range(5)]
    for l in range(depth):
        g_attn = norm_attn_g[l].reshape(1, d)
        g_ffn = norm_ffn_g[l].reshape(1, d)

        u, q, k, v, kb, vb, qi, kw, ki2 = _inproj(xp, g_attn, mod_p(l, 1), mod_p(l, 0), w_aug[l], tm=tm_p, mod_rows=SUBLANES)
        pool = _pool_prompt(u, w_pool_b[l], pool_scale[l], tp=tm_p, seq=seq)
        sh3 = lambda a: a.reshape(bp, seq, a.shape[-1])
        attn = _attn_prompt(sh3(q), sh3(qi), sh3(kw), sh3(ki2), sh3(kb), sh3(vb), bias_p, tq=tq, topk=topk_p)
        x1 = _outproj(xp, pool, attn.reshape(bp * seq, ATTN_WIDTH), mod_p(l, 2), w_out_b[l], tm=tm_p, mod_rows=SUBLANES)
        xp, tail_a, tail_g = _ffn(x1, g_ffn, mod_p(l, 4), mod_p(l, 3), mod_p(l, 5), w_up_b[l], conv_pack[l], w_down_b[l],
                                  None, tm=tm_p, tf=1408, shift=1, halo=SUBLANES, seq_rows=seq, mod_rows=SUBLANES)
        tails = jnp.concatenate([tail_a, tail_g], axis=2).reshape(bp, seq // tm_p, SUBLANES, 2 * D_FF)
        outs_p[0].append(k.reshape(bp, seq, N_HEADS, HEAD_DIM))
        outs_p[1].append(v.reshape(bp, seq, N_HEADS, HEAD_DIM))
        outs_p[2].append(kw[:, :IDX_DIM].reshape(bp, seq, IDX_DIM))
        outs_p[3].append(u.reshape(bp, seq, POOL_WIDTH)[:, seq - POOL_STATE:])
        outs_p[4].append(tails[:, -1, SUBLANES - (CONV_WIDTH - 1):])

        u, q, k, v, kb, vb, qi, kw, ki2 = _inproj(xs, g_attn, mod_s(l, 1), mod_s(l, 0), w_aug[l], tm=tm_s, mod_rows=tm_s)
        u_bt = to_bt(u)
        pool_state = jnp.concatenate([state_pool[l], u_bt], axis=1)
        pool = _pool_sample(jnp.swapaxes(pool_state, 0, 1), w_pool_b[l], pool_scale[l], t_new=t_new, pos0=past)
        pool = pool.reshape(ns, POOL_WIDTH)
        qi_bt = to_bt(qi.astype(F32)).reshape(nb, t_new, N_IDX_HEADS, IDX_DIM)
        qi_bd = jnp.pad(jnp.swapaxes(qi_bt, 1, 2), ((0, 0), (0, 0), (0, SUBLANES - t_new), (0, LANES - IDX_DIM)))
        qi_bd = qi_bd.reshape(nb, N_IDX_HEADS * SUBLANES, LANES).astype(BF16)
        kw_bt = to_bt(kw)
        wi_col = jnp.pad(jnp.swapaxes(kw_bt[:, :, IDX_DIM:IDX_DIM + N_IDX_HEADS], 1, 2), ((0, 0), (0, 0), (0, SUBLANES - t_new)))
        wi_col = wi_col.reshape(nb, N_IDX_HEADS * SUBLANES, 1)
        new_rows = 2 * SUBLANES
        attn = _attn_sample(page_table, qi_bd, wi_col, pad_t(to_bt(q.astype(F32)), SUBLANES),
                            pad_t(kw_bt, new_rows), pad_t(to_bt(k), new_rows), pad_t(to_bt(v), new_rows), bias_s,
                            cache_idx_k, cache_k2, cache_v2, layer=l, t_new=t_new, topk=topk_s)
        attn = jnp.swapaxes(attn[:, :t_new], 0, 1).reshape(ns, ATTN_WIDTH).astype(BF16)
        x1 = _outproj(xs, pool, attn, mod_s(l, 2), w_out_b[l], tm=tm_s, mod_rows=tm_s)
        conv_prev = jnp.swapaxes(state_conv[l], 0, 1).reshape((CONV_WIDTH - 1) * nb, 2 * D_FF)
        xs, tail_a, tail_g = _ffn(x1, g_ffn, mod_s(l, 4), mod_s(l, 3), mod_s(l, 5), w_up_b[l], conv_pack[l], w_down_b[l],
                                  conv_prev, tm=tm_s, tf=1408, shift=nb, halo=(CONV_WIDTH - 1) * nb, seq_rows=ns, mod_rows=tm_s)
        conv_state = jnp.concatenate([tail_a, tail_g], axis=2).reshape(CONV_WIDTH - 1, nb, 2 * D_FF)
        outs_s[0].append(to_bt(k).reshape(nb, t_new, N_HEADS, HEAD_DIM))
        outs_s[1].append(to_bt(v).reshape(nb, t_new, N_HEADS, HEAD_DIM))
        outs_s[2].append(kw_bt[:, :, :IDX_DIM])
        outs_s[3].append(pool_state[:, -POOL_STATE:])
        outs_s[4].append(jnp.swapaxes(conv_state, 0, 1))

    y_prompt = _final_norm(xp, final_norm_g, tm=tm_p).reshape(bp, seq, d)
    y_sample = to_bt(_final_norm(xs, final_norm_g, tm=tm_s))
    return (y_prompt, y_sample) + tuple(jnp.stack(o) for o in outs_p) + tuple(jnp.stack(o) for o in outs_s)
```

```python
import functools

import numpy as np
import jax
import jax.numpy as jnp
from jax import lax
from jax.experimental import pallas as pl
from jax.experimental.pallas import tpu as pltpu

F32 = jnp.float32
BF16 = jnp.bfloat16

LANES = 128
SUBLANES = 8

D_MODEL = 1024
DEPTH = 4
PAGE_SIZE = 128
POOL_WIDTH = D_MODEL // 2
POOL_WINDOWS = (2, 4, 8, 16)
POOL_GROUP = POOL_WIDTH // len(POOL_WINDOWS)
POOL_STATE = max(POOL_WINDOWS) - 1
POOL_HALO = 16
N_HEADS = 8
HEAD_DIM = 64
ATTN_WIDTH = N_HEADS * HEAD_DIM
N_IDX_HEADS = 8
IDX_DIM = 64
TOPK_MAX = 256
N_BUCKETS = 32
MAX_DISTANCE = 128
D_FF = 2816
CONV_WIDTH = 3
EPS = 1e-6

COL_U, COL_Q, COL_K, COL_V, COL_QI, COL_KW, COL_KI2, COL_END = 0, 512, 1024, 1536, 2048, 2560, 2688, 2816
IN_WIDTH = 2632

NEG_BIG = -1e30
BISECT_ITERS = 40
TINY = 1e-30
NEW_ROWS = 2 * SUBLANES
SELECT_GROUP = 8

MIB = 1024 * 1024


def _params(sem, vmem_mib):
    return pltpu.CompilerParams(dimension_semantics=sem, vmem_limit_bytes=vmem_mib * MIB)


def _rows(ref, tm):
    v = ref[...]
    return v if v.shape[0] == tm else v[0:1]


def _norm_mod(x, g, sc, sh):
    y = x * lax.rsqrt(jnp.mean(x * x, axis=-1, keepdims=True) + EPS)
    return (y * g) * (1.0 + sc) + sh


def _silu(x):
    return x * (1.0 / (1.0 + jnp.exp(-x)))


def _dot_t(a, b):
    return lax.dot_general(a, b, (((1,), (1,)), ((), ())), preferred_element_type=F32)


def _fold(x):
    return x.reshape(x.shape[0] // SUBLANES, SUBLANES, x.shape[1])


def _mod_kernel(c_ref, w_ref, b_ref, o_ref):
    a = _silu(c_ref[...]).astype(BF16)
    o_ref[0] = jnp.dot(a, w_ref[0], preferred_element_type=F32) + b_ref[0]


def _modulation(c_all, w_mod_b, b_mod):
    n, d = c_all.shape
    depth, _, width = w_mod_b.shape
    tn = 1536
    return pl.pallas_call(
        _mod_kernel,
        grid=(depth, width // tn),
        in_specs=[pl.BlockSpec((n, d), lambda l, j: (0, 0)),
                  pl.BlockSpec((1, d, tn), lambda l, j: (l, 0, j)),
                  pl.BlockSpec((1, 1, tn), lambda l, j: (l, 0, j))],
        out_specs=pl.BlockSpec((1, n, tn), lambda l, j: (l, 0, j)),
        out_shape=jax.ShapeDtypeStruct((depth, n, width), F32),
        compiler_params=_params(("arbitrary", "arbitrary"), 32),
        name="modulation",
    )(c_all, w_mod_b, b_mod.reshape(depth, 1, width))


def _bias_table_kernel(rb_ref, bk_ref, o_ref):
    h = pl.program_id(0)
    bk = bk_ref[...]
    acc = jnp.zeros(bk.shape, F32)
    for b in range(N_BUCKETS):
        acc = jnp.where(bk == b, rb_ref[b * N_HEADS + h], acc)
    o_ref[0] = acc - rb_ref[(N_BUCKETS - 1) * N_HEADS + h]


def _bias_table(rel_bias, buckets):
    r, n = buckets.shape
    return pl.pallas_call(
        _bias_table_kernel,
        grid=(N_HEADS,),
        in_specs=[pl.BlockSpec(memory_space=pltpu.SMEM),
                  pl.BlockSpec((r, n), lambda h: (0, 0))],
        out_specs=pl.BlockSpec((1, r, n), lambda h: (h, 0, 0)),
        out_shape=jax.ShapeDtypeStruct((N_HEADS, r, n), F32),
        compiler_params=_params(("arbitrary",), 32),
        name="bias_table",
    )(rel_bias.reshape(-1), buckets)


def _t5_bucket_np(n):
    max_exact = N_BUCKETS // 2
    nf = np.maximum(n, 1).astype(np.float64)
    large = max_exact + (np.log(nf / max_exact) / np.log(MAX_DISTANCE / max_exact) * (N_BUCKETS - max_exact)).astype(np.int64)
    large = np.minimum(large, N_BUCKETS - 1)
    return np.where(n < max_exact, n, large).astype(np.int32)


def _inproj_kernel(x_ref, g_ref, sc_ref, sh_ref, w_ref,
                   u_ref, q_ref, k_ref, v_ref, kb_ref, vt_ref, qi_ref, kw_ref, ki2_ref, *, tm, tk):
    h = _norm_mod(x_ref[...], g_ref[...], _rows(sc_ref, tm), _rows(sh_ref, tm)).astype(BF16)

    def mm(c0, c1):
        return jnp.dot(h, w_ref[:, c0:c1], preferred_element_type=F32)

    u_ref[...] = mm(COL_U, COL_Q)
    q_ref[...] = (mm(COL_Q, COL_K) * (HEAD_DIM ** -0.5)).astype(BF16)
    k = mm(COL_K, COL_V)
    k_ref[...] = k
    kb_ref[...] = k.astype(BF16)
    v = mm(COL_V, COL_QI)
    v_ref[...] = v
    v_t = v.T
    for c in range(tm // tk):
        vt_ref[c] = v_t[:, c * tk:(c + 1) * tk].astype(BF16)
    qi_ref[...] = (mm(COL_QI, COL_KW) * (IDX_DIM ** -0.5)).astype(BF16)
    kw = mm(COL_KW, COL_KI2)
    lane = lax.broadcasted_iota(jnp.int32, kw.shape, 1)
    kw_ref[...] = jnp.where(lane >= IDX_DIM, kw * (N_IDX_HEADS ** -0.5), kw)
    ki2_ref[...] = mm(COL_KI2, COL_END).astype(BF16)


def _mod_spec(n, tm, d, mod_rows, arr):
    if mod_rows == tm:
        return pl.BlockSpec((tm, d), lambda i, *_: (i, 0))
    per = (n // tm) // (arr.shape[0] // SUBLANES)
    return pl.BlockSpec((SUBLANES, d), lambda i, *_: (i // per, 0))


def _inproj(x, g, sc, sh, w_aug, *, tm, tk, mod_rows):
    n, d = x.shape
    mod_spec = _mod_spec(n, tm, d, mod_rows, sc)
    row = lambda w: pl.BlockSpec((tm, w), lambda i: (i, 0))
    outs = [(POOL_WIDTH, F32), (ATTN_WIDTH, BF16), (ATTN_WIDTH, F32), (ATTN_WIDTH, F32), (ATTN_WIDTH, BF16),
            None, (N_IDX_HEADS * IDX_DIM, BF16), (LANES, F32), (LANES, BF16)]
    vt_spec = pl.BlockSpec((tm // tk, ATTN_WIDTH, tk), lambda i: (i, 0, 0))
    vt_shape = jax.ShapeDtypeStruct((n // tk, ATTN_WIDTH, tk), BF16)
    return pl.pallas_call(
        functools.partial(_inproj_kernel, tm=tm, tk=tk),
        grid=(n // tm,),
        in_specs=[row(d), pl.BlockSpec((1, d), lambda i: (0, 0)), mod_spec, mod_spec,
                  pl.BlockSpec((d, COL_END), lambda i: (0, 0))],
        out_specs=[vt_spec if o is None else row(o[0]) for o in outs],
        out_shape=[vt_shape if o is None else jax.ShapeDtypeStruct((n, o[0]), o[1]) for o in outs],
        compiler_params=_params(("arbitrary",), 48),
        name="inproj",
    )(x, g, sc, sh, w_aug)


def _pool_group(s, cnt, u_g, w_g, ps_g):
    d = s / cnt - u_g
    return (jnp.dot(d.astype(BF16), w_g, preferred_element_type=F32) * ps_g).astype(BF16)


def _pool_prompt_kernel(u_ref, halo_ref, w_ref, ps_ref, o_ref, ext_ref, *, tp, tiles_per_seq):
    i = pl.program_id(0) % tiles_per_seq
    ext_ref[0:POOL_HALO, :] = jnp.where(i == 0, 0.0, halo_ref[...])
    ext_ref[POOL_HALO:, :] = u_ref[...]
    pos = i * tp + lax.broadcasted_iota(jnp.int32, (tp, POOL_GROUP), 0)
    for g, w in enumerate(POOL_WINDOWS):
        sl = slice(g * POOL_GROUP, (g + 1) * POOL_GROUP)
        s = ext_ref[POOL_HALO:POOL_HALO + tp, sl]
        for j in range(1, w):
            s = s + ext_ref[POOL_HALO - j:POOL_HALO - j + tp, sl]
        cnt = jnp.minimum(w, pos + 1).astype(F32)
        o_ref[:, sl] = _pool_group(s, cnt, u_ref[:, sl], w_ref[g], ps_ref[:, sl])


def _pool_prompt(u, w_pool_b, pool_scale, *, tp, seq):
    n, p = u.shape
    tiles_per_seq = seq // tp
    hb = tp // POOL_HALO
    return pl.pallas_call(
        functools.partial(_pool_prompt_kernel, tp=tp, tiles_per_seq=tiles_per_seq),
        grid=(n // tp,),
        in_specs=[pl.BlockSpec((tp, p), lambda i: (i, 0)),
                  pl.BlockSpec((POOL_HALO, p), lambda i: (jnp.maximum(i * hb - 1, 0), 0)),
                  pl.BlockSpec(w_pool_b.shape, lambda i: (0, 0, 0)),
                  pl.BlockSpec((1, p), lambda i: (0, 0))],
        out_specs=pl.BlockSpec((tp, p), lambda i: (i, 0)),
        out_shape=jax.ShapeDtypeStruct((n, p), BF16),
        scratch_shapes=[pltpu.VMEM((tp + POOL_HALO, p), F32)],
        compiler_params=_params(("arbitrary",), 32),
        name="pool_prompt",
    )(u, u, w_pool_b, pool_scale.reshape(1, p))


def _pool_sample_kernel(ext_ref, w_ref, ps_ref, o_ref, *, t_new, pos0):
    for t in range(t_new):
        for g, w in enumerate(POOL_WINDOWS):
            sl = slice(g * POOL_GROUP, (g + 1) * POOL_GROUP)
            s = ext_ref[POOL_STATE + t, :, sl]
            for j in range(1, w):
                s = s + ext_ref[POOL_STATE + t - j, :, sl]
            cnt = float(min(w, pos0 + t + 1))
            o_ref[t, :, sl] = _pool_group(s, cnt, ext_ref[POOL_STATE + t, :, sl], w_ref[g], ps_ref[:, sl])


def _pool_sample(ext_tm, w_pool_b, pool_scale, *, t_new, pos0):
    _, nb, p = ext_tm.shape
    return pl.pallas_call(
        functools.partial(_pool_sample_kernel, t_new=t_new, pos0=pos0),
        out_shape=jax.ShapeDtypeStruct((t_new, nb, p), BF16),
        compiler_params=pltpu.CompilerParams(vmem_limit_bytes=32 * MIB),
        name="pool_sample",
    )(ext_tm, w_pool_b, pool_scale.reshape(1, p))


def _topk_threshold(count_ge, lo0, hi0, clo0, active, fk):
    def unresolved(clo):
        return jnp.max(jnp.where(jnp.logical_and(clo != fk, active), 1, 0))

    def cond(carry):
        it, _, _, _, pending = carry
        return jnp.logical_and(it < BISECT_ITERS, pending > 0)

    def body(carry):
        it, lo, hi, clo, _ = carry
        mid = 0.5 * (lo + hi)
        c = count_ge(mid)
        ge = c >= fk
        lo = jnp.where(ge, mid, lo)
        clo = jnp.where(ge, c, clo)
        hi = jnp.where(ge, hi, mid)
        return it + 1, lo, hi, clo, unresolved(clo)

    _, lo, hi, clo, pending = lax.while_loop(cond, body, (jnp.int32(0), lo0, hi0, clo0, unresolved(clo0)))
    return lo, hi, clo, pending


def _tie_cut(count_group_below, need, idx_bits, like):
    def body(s, jb):
        cand = jb + jnp.left_shift(1, idx_bits - 1 - s)
        return jnp.where(count_group_below(cand) < need, cand, jb)

    return lax.fori_loop(0, idx_bits, body, jnp.zeros(like.shape, jnp.int32))


def _attn_prompt_kernel(q_ref, qi_ref, kw_ref, ki2_ref, k_ref, vt_ref, bias_ref, o_ref,
                        sc_ref, qm_ref, qim_ref, wt_ref, m_ref, l_ref, acc_ref, lg_ref, *, tq, topk, idx_bits):
    i = pl.program_id(1)
    nk = i + 1
    low_half = lax.broadcasted_iota(jnp.int32, (tq, LANES), 1) < HEAD_DIM

    for hp in range(N_HEADS // 2):
        sl = slice(hp * LANES, (hp + 1) * LANES)
        for ref, src in ((qm_ref, q_ref), (qim_ref, qi_ref)):
            x = src[0, :, sl]
            zero = jnp.zeros_like(x)
            ref[2 * hp] = jnp.where(low_half, x, zero)
            ref[2 * hp + 1] = jnp.where(low_half, zero, x)
    wt_ref[...] = kw_ref[0].T[IDX_DIM:IDX_DIM + N_IDX_HEADS, :]

    krow = lax.broadcasted_iota(jnp.int32, (tq, tq), 0)
    qcol = lax.broadcasted_iota(jnp.int32, (tq, tq), 1)

    def score_tile(t, diag):
        kt = ki2_ref[0, pl.ds(pl.multiple_of(t * tq, tq), tq), :]
        s_acc = jnp.zeros((tq, tq), F32)
        for h in range(N_IDX_HEADS):
            s = _dot_t(kt, qim_ref[h])
            s_acc = s_acc + wt_ref[h:h + 1, :] * jnp.maximum(s, 0.0)
        if diag:
            s_acc = jnp.where(krow <= qcol, s_acc, -jnp.inf)
        sc_ref[t] = s_acc

    def score_body(t, c):
        score_tile(t, False)
        return c

    lax.fori_loop(0, i, score_body, 0)
    score_tile(i, True)

    def reduce_tiles(fn, init):
        return lax.fori_loop(0, nk, lambda t, acc: fn(acc, _fold(sc_ref[t]), t), init)

    def count_ge(thr):
        acc = reduce_tiles(lambda a, x, t: a + jnp.sum(jnp.where(x >= thr, 1.0, 0.0), axis=0),
                           jnp.zeros((SUBLANES, tq), F32))
        return jnp.sum(acc, axis=0, keepdims=True)

    qpos = i * tq + lax.broadcasted_iota(jnp.int32, (1, tq), 1)
    short = qpos < topk
    active = jnp.logical_not(short)
    fk = float(topk)

    mx = reduce_tiles(lambda a, x, t: jnp.maximum(a, jnp.max(x, axis=0)), jnp.full((SUBLANES, tq), -jnp.inf, F32))
    mn = reduce_tiles(lambda a, x, t: jnp.minimum(a, jnp.min(jnp.where(x == -jnp.inf, jnp.inf, x), axis=0)),
                      jnp.full((SUBLANES, tq), jnp.inf, F32))
    mx = jnp.max(mx, axis=0, keepdims=True)
    lo0 = jnp.min(mn, axis=0, keepdims=True)
    hi0 = mx + (jnp.abs(mx) * 2.0 ** -20 + TINY)
    clo0 = jnp.where(short, fk, (qpos + 1).astype(F32))
    lo, hi, clo, pending = _topk_threshold(count_ge, lo0, hi0, clo0, active, fk)

    @pl.when(pending > 0)
    def _resolve_ties():
        tie_q = jnp.logical_and(clo != fk, active)
        need = fk - count_ge(hi)
        krow3 = _fold(krow)

        def in_group(x):
            return jnp.logical_and(x >= lo, x < hi)

        def count_group_below(bound):
            acc = reduce_tiles(
                lambda a, x, t: a + jnp.sum(
                    jnp.where(jnp.logical_and(in_group(x), krow3 + t * tq < bound), 1.0, 0.0), axis=0),
                jnp.zeros((SUBLANES, tq), F32))
            return jnp.sum(acc, axis=0, keepdims=True)

        cut = _tie_cut(count_group_below, need, idx_bits, lo)
        cut = jnp.where(tie_q, cut, jnp.int32(2 ** 30))

        def drop_body(t, c):
            x = sc_ref[t]
            drop = jnp.logical_and(in_group(x), krow + t * tq > cut)
            sc_ref[t] = jnp.where(drop, -jnp.inf, x)
            return c

        lax.fori_loop(0, nk, drop_body, 0)

    lo = jnp.where(short, NEG_BIG, lo)

    m_ref[...] = jnp.full(m_ref.shape, NEG_BIG, F32)
    l_ref[...] = jnp.zeros(l_ref.shape, F32)
    acc_ref[...] = jnp.zeros(acc_ref.shape, F32)

    def attend(t, bias_sel):
        kt = k_ref[0, pl.ds(pl.multiple_of(t * tq, tq), tq), :]
        off = jnp.where(sc_ref[t] >= lo, 0.0, NEG_BIG)
        for h in range(N_HEADS):
            lg_ref[h] = _dot_t(kt[:, (h // 2) * LANES:(h // 2 + 1) * LANES], qm_ref[h])
        for h in range(N_HEADS):
            lg = lg_ref[h]
            if bias_sel is not None:
                lg = lg + bias_ref[h, :, bias_sel * tq:(bias_sel + 1) * tq]
            lg = lg + off
            m_old = m_ref[h]
            m_new = jnp.maximum(m_old, jnp.max(jnp.max(_fold(lg), axis=0), axis=0, keepdims=True))
            p = jnp.exp(lg - m_new)
            alpha = jnp.exp(m_old - m_new)
            l_ref[h] = alpha * l_ref[h] + jnp.sum(jnp.sum(_fold(p), axis=0), axis=0, keepdims=True)
            v_h = vt_ref[t, h * HEAD_DIM:(h + 1) * HEAD_DIM, :]
            acc_ref[h] = alpha * acc_ref[h] + jnp.dot(v_h, p.astype(BF16), preferred_element_type=F32)
            m_ref[h] = m_new

    def attend_body(t, c):
        attend(t, None)
        return c

    lax.fori_loop(0, jnp.maximum(i - 1, 0), attend_body, 0)

    @pl.when(i >= 1)
    def _near():
        attend(i - 1, 1)

    attend(i, 0)

    out_t = jnp.concatenate([acc_ref[h] / l_ref[h] for h in range(N_HEADS)], axis=0)
    o_ref[0] = out_t.T.astype(BF16)


def _attn_prompt(q, qi, kw, ki2, kb, vt, bias, *, tq, topk):
    b, t, _ = q.shape
    assert t % tq == 0 and tq % LANES == 0 and tq >= topk
    nt = t // tq
    tile = lambda w: pl.BlockSpec((1, tq, w), lambda bi, i: (bi, i, 0))
    whole = lambda w: pl.BlockSpec((1, t, w), lambda bi, i: (bi, 0, 0), pipeline_mode=pl.Buffered(1))
    return pl.pallas_call(
        functools.partial(_attn_prompt_kernel, tq=tq, topk=topk, idx_bits=max(1, int(np.ceil(np.log2(t))))),
        grid=(b, nt),
        in_specs=[tile(ATTN_WIDTH), tile(N_IDX_HEADS * IDX_DIM), tile(LANES), whole(LANES), whole(ATTN_WIDTH),
                  pl.BlockSpec((nt, ATTN_WIDTH, tq), lambda bi, i: (bi, 0, 0), pipeline_mode=pl.Buffered(1)),
                  pl.BlockSpec(bias.shape, lambda bi, i: (0, 0, 0), pipeline_mode=pl.Buffered(1))],
        out_specs=tile(ATTN_WIDTH),
        out_shape=jax.ShapeDtypeStruct((b, t, ATTN_WIDTH), BF16),
        scratch_shapes=[pltpu.VMEM((nt, tq, tq), F32),
                        pltpu.VMEM((N_HEADS, tq, LANES), BF16),
                        pltpu.VMEM((N_IDX_HEADS, tq, LANES), BF16),
                        pltpu.VMEM((N_IDX_HEADS, tq), F32),
                        pltpu.VMEM((N_HEADS, 1, tq), F32),
                        pltpu.VMEM((N_HEADS, 1, tq), F32),
                        pltpu.VMEM((N_HEADS, HEAD_DIM, tq), F32),
                        pltpu.VMEM((N_HEADS, tq, tq), F32)],
        compiler_params=_params(("arbitrary", "arbitrary"), 56),
        name="attn_prompt",
    )(q, qi, kw, ki2, kb, vt, bias)


def _select_sample_kernel(pt_ref, qi_ref, wi_ref, kin_ref, cidx_hbm, o_ref, kie, sems,
                          *, layer, n_pages, t_new, topk, idx_bits, group):
    s_id = pl.program_id(0)
    ns = pl.num_programs(0)
    slot = s_id % 2
    past = n_pages * PAGE_SIZE
    n_keys = kie.shape[2]

    def page_copies(step, sl):
        out = []
        for g in range(group):
            for p in range(n_pages):
                pg = pt_ref[step * group + g, p]
                out.append(pltpu.make_async_copy(cidx_hbm.at[layer, pg],
                                                 kie.at[sl, g, pl.ds(p * PAGE_SIZE, PAGE_SIZE)], sems.at[sl]))
        return out

    @pl.when(s_id == 0)
    def _first():
        for c in page_copies(0, 0):
            c.start()

    @pl.when(s_id + 1 < ns)
    def _prefetch():
        for c in page_copies(s_id + 1, 1 - slot):
            c.start()

    for c in page_copies(s_id, slot):
        c.wait()

    new = kin_ref.shape[1]
    scores = []
    for g in range(group):
        kie[slot, g, past:past + new, :] = kin_ref[g]
        kie[slot, g, past + new:, :] = jnp.zeros((n_keys - past - new, LANES), BF16)
        s = _dot_t(qi_ref[g], kie[slot, g])
        wi = wi_ref[g]
        sc = jnp.zeros((SUBLANES, n_keys), F32)
        for h in range(N_IDX_HEADS):
            rs = slice(h * SUBLANES, (h + 1) * SUBLANES)
            sc = sc + wi[rs] * jnp.maximum(s[rs], 0.0)
        scores.append(sc)
    score = jnp.concatenate(scores, axis=0)
    rows = group * SUBLANES
    t_row = lax.broadcasted_iota(jnp.int32, (rows, n_keys), 0) % SUBLANES
    col = lax.broadcasted_iota(jnp.int32, (rows, n_keys), 1)
    adm = jnp.logical_or(col < past, jnp.logical_and(col - past <= t_row, col < past + t_new))
    score = jnp.where(adm, score, -jnp.inf)

    fk = float(topk)

    def count_ge(thr):
        return jnp.sum(jnp.where(score >= thr, 1.0, 0.0), axis=1, keepdims=True)

    mx = jnp.max(score, axis=1, keepdims=True)
    lo0 = jnp.min(jnp.where(adm, score, jnp.inf), axis=1, keepdims=True)
    hi0 = mx + (jnp.abs(mx) * 2.0 ** -20 + TINY)
    n_adm = past + 1 + jnp.minimum(lax.broadcasted_iota(jnp.int32, (rows, 1), 0) % SUBLANES, t_new - 1)
    active = jnp.full((rows, 1), True)
    lo, hi, _, _ = _topk_threshold(count_ge, lo0, hi0, n_adm.astype(F32), active, fk)
    need = fk - count_ge(hi)
    in_group = jnp.logical_and(score >= lo, score < hi)
    cut = _tie_cut(lambda b: jnp.sum(jnp.where(jnp.logical_and(in_group, col < b), 1.0, 0.0), axis=1, keepdims=True),
                   need, idx_bits, lo)
    sel = jnp.logical_or(score >= hi, jnp.logical_and(in_group, col <= cut))
    o_ref[...] = jnp.where(sel, 0.0, NEG_BIG).reshape(group, SUBLANES, n_keys)


def _select_sample(page_table, qi_bd, wi_col, kin, cache_ib, *, layer, t_new, topk):
    nb, n_pages = page_table.shape
    past = n_pages * PAGE_SIZE
    assert past + 1 > topk and nb % SELECT_GROUP == 0
    n_keys = past + LANES
    g = SELECT_GROUP
    per = lambda shape: pl.BlockSpec((g,) + shape, lambda s, pt: (s, 0, 0))
    grid_spec = pltpu.PrefetchScalarGridSpec(
        num_scalar_prefetch=1,
        grid=(nb // g,),
        in_specs=[per((N_IDX_HEADS * SUBLANES, LANES)), per((N_IDX_HEADS * SUBLANES, 1)), per((NEW_ROWS, LANES)),
                  pl.BlockSpec(memory_space=pl.ANY)],
        out_specs=per((SUBLANES, n_keys)),
        scratch_shapes=[pltpu.VMEM((2, g, n_keys, LANES), BF16), pltpu.SemaphoreType.DMA((2,))],
    )
    return pl.pallas_call(
        functools.partial(_select_sample_kernel, layer=layer, n_pages=n_pages, t_new=t_new, topk=topk,
                          idx_bits=max(1, int(np.ceil(np.log2(n_keys)))), group=g),
        grid_spec=grid_spec,
        out_shape=jax.ShapeDtypeStruct((nb, SUBLANES, n_keys), F32),
        compiler_params=_params(("arbitrary",), 48),
        name="select_sample",
    )(page_table, qi_bd, wi_col, kin, cache_ib)


def _attn_sample_kernel(pt_ref, q_ref, off_ref, kn_ref, vn_ref, bias_ref, ck_hbm, cv_hbm, o_ref,
                        ke, ve, sems, *, layer, n_pages):
    b = pl.program_id(0)
    nb = pl.num_programs(0)
    slot = b % 2
    past = n_pages * PAGE_SIZE
    n_keys = ke.shape[1]

    def page_copies(bb, sl):
        out = []
        for p in range(n_pages):
            pg = pt_ref[bb, p]
            rows = pl.ds(p * PAGE_SIZE, PAGE_SIZE)
            out.append(pltpu.make_async_copy(ck_hbm.at[layer, pg], ke.at[sl, rows], sems.at[0, sl]))
            out.append(pltpu.make_async_copy(cv_hbm.at[layer, pg], ve.at[sl, rows], sems.at[1, sl]))
        return out

    @pl.when(b == 0)
    def _first():
        for c in page_copies(0, 0):
            c.start()

    @pl.when(b + 1 < nb)
    def _prefetch():
        for c in page_copies(b + 1, 1 - slot):
            c.start()

    for c in page_copies(b, slot):
        c.wait()

    new = kn_ref.shape[1]
    ke[slot, past:past + new, :] = kn_ref[0]
    ve[slot, past:past + new, :] = vn_ref[0]
    ke[slot, past + new:, :] = jnp.zeros((n_keys - past - new, ATTN_WIDTH), BF16)
    ve[slot, past + new:, :] = jnp.zeros((n_keys - past - new, ATTN_WIDTH), BF16)

    hrow = lax.broadcasted_iota(jnp.int32, (N_HEADS * SUBLANES, ATTN_WIDTH), 0) // SUBLANES
    hlane = lax.broadcasted_iota(jnp.int32, (N_HEADS * SUBLANES, ATTN_WIDTH), 1) // HEAD_DIM
    own = hrow == hlane
    qbd = jnp.where(own, jnp.concatenate([q_ref[0]] * N_HEADS, axis=0), 0.0).astype(BF16)
    lg = _dot_t(qbd, ke[slot]) + bias_ref[...] + jnp.concatenate([off_ref[0]] * N_HEADS, axis=0)
    m = jnp.max(lg, axis=1, keepdims=True)
    p = jnp.exp(lg - m)
    den = jnp.sum(p, axis=1, keepdims=True)
    o = jnp.dot(p.astype(BF16), ve[slot], preferred_element_type=F32) / den
    o = jnp.where(own, o, 0.0)
    out = o[0:SUBLANES]
    for h in range(1, N_HEADS):
        out = out + o[h * SUBLANES:(h + 1) * SUBLANES]
    o_ref[0] = out


def _attn_sample(page_table, q8, off, kn, vn, bias, cache_kb, cache_vb, *, layer):
    nb, n_pages = page_table.shape
    n_keys = n_pages * PAGE_SIZE + LANES
    per_b = lambda shape: pl.BlockSpec((1,) + shape, lambda b, pt: (b, 0, 0))
    grid_spec = pltpu.PrefetchScalarGridSpec(
        num_scalar_prefetch=1,
        grid=(nb,),
        in_specs=[per_b((SUBLANES, ATTN_WIDTH)), per_b((SUBLANES, n_keys)),
                  per_b((NEW_ROWS, ATTN_WIDTH)), per_b((NEW_ROWS, ATTN_WIDTH)),
                  pl.BlockSpec(bias.shape, lambda b, pt: (0, 0)),
                  pl.BlockSpec(memory_space=pl.ANY), pl.BlockSpec(memory_space=pl.ANY)],
        out_specs=per_b((SUBLANES, ATTN_WIDTH)),
        scratch_shapes=[pltpu.VMEM((2, n_keys, ATTN_WIDTH), BF16),
                        pltpu.VMEM((2, n_keys, ATTN_WIDTH), BF16),
                        pltpu.SemaphoreType.DMA((2, 2))],
    )
    return pl.pallas_call(
        functools.partial(_attn_sample_kernel, layer=layer, n_pages=n_pages),
        grid_spec=grid_spec,
        out_shape=jax.ShapeDtypeStruct((nb, SUBLANES, ATTN_WIDTH), F32),
        compiler_params=_params(("arbitrary",), 48),
        name="attn_sample",
    )(page_table, q8, off, kn, vn, bias, cache_kb, cache_vb)


def _outproj_kernel(x_ref, p_ref, a_ref, g_ref, w_ref, o_ref, *, tm):
    mixed = (jnp.dot(p_ref[...], w_ref[0:POOL_WIDTH, :], preferred_element_type=F32)
             + jnp.dot(a_ref[...], w_ref[POOL_WIDTH:, :], preferred_element_type=F32))
    o_ref[...] = x_ref[...] + _rows(g_ref, tm) * mixed


def _outproj(x, pool, attn, gate, w_out_b, *, tm, mod_rows):
    n, d = x.shape
    row = lambda w: pl.BlockSpec((tm, w), lambda i: (i, 0))
    return pl.pallas_call(
        functools.partial(_outproj_kernel, tm=tm),
        grid=(n // tm,),
        in_specs=[row(d), row(POOL_WIDTH), row(ATTN_WIDTH), _mod_spec(n, tm, d, mod_rows, gate),
                  pl.BlockSpec(w_out_b.shape, lambda i: (0, 0))],
        out_specs=row(d),
        out_shape=jax.ShapeDtypeStruct((n, d), F32),
        compiler_params=_params(("arbitrary",), 32),
        name="outproj",
    )(x, pool, attn, gate, w_out_b)


def _ffn_kernel(*refs, tm, tf, shift, halo, tiles_per_seq, from_state):
    if from_state:
        (x_ref, gn_ref, sc_ref, sh_ref, gate_ref, wa_ref, wg_ref, ca_ref, cg_ref, wd_ref, sa_ref, sg_ref,
         o_ref, ta_ref, tg_ref, h_ref, acc_ref, ea_ref, eg_ref) = refs
    else:
        (x_ref, gn_ref, sc_ref, sh_ref, gate_ref, wa_ref, wg_ref, ca_ref, cg_ref, wd_ref,
         o_ref, ta_ref, tg_ref, h_ref, acc_ref, ea_ref, eg_ref, cra_ref, crg_ref) = refs
    i = pl.program_id(0)
    j = pl.program_id(1)

    @pl.when(j == 0)
    def _start():
        h_ref[...] = _norm_mod(x_ref[...], gn_ref[...], _rows(sc_ref, tm), _rows(sh_ref, tm)).astype(BF16)
        acc_ref[...] = jnp.zeros(acc_ref.shape, F32)

    h = h_ref[...]

    def conv_half(w_ref, c_ref, e_ref, state_ref, carry_ref, tail_ref):
        up = jnp.dot(h, w_ref[...], preferred_element_type=F32)
        if from_state:
            e_ref[0:halo, :] = state_ref[...]
        else:
            e_ref[0:halo, :] = jnp.where(i % tiles_per_seq == 0, 0.0, carry_ref[j])
        e_ref[halo:, :] = up
        y = c_ref[3:4, :] + c_ref[0:1, :] * e_ref[halo - 2 * shift:halo - 2 * shift + tm, :]
        y = y + c_ref[1:2, :] * e_ref[halo - shift:halo - shift + tm, :]
        y = y + c_ref[2:3, :] * up
        tail = e_ref[tm:tm + halo, :]
        if not from_state:
            carry_ref[j] = tail
        tail_ref[0] = tail
        return y

    if from_state:
        a = conv_half(wa_ref, ca_ref, ea_ref, sa_ref, None, ta_ref)
        g = conv_half(wg_ref, cg_ref, eg_ref, sg_ref, None, tg_ref)
    else:
        a = conv_half(wa_ref, ca_ref, ea_ref, None, cra_ref, ta_ref)
        g = conv_half(wg_ref, cg_ref, eg_ref, None, crg_ref, tg_ref)
    act = (_silu(g) * a).astype(BF16)
    acc_ref[...] += jnp.dot(act, wd_ref[...], preferred_element_type=F32)

    @pl.when(j == pl.num_programs(1) - 1)
    def _finish():
        o_ref[...] = x_ref[...] + _rows(gate_ref, tm) * acc_ref[...]


def _ffn(x, gn, sc, sh, gate, w_up_b, conv_pack, w_down_b, state, *, tm, tf, shift, halo, seq_rows, mod_rows):
    n, d = x.shape
    nff = D_FF // tf
    ni = n // tm
    from_state = state is not None
    assert (not from_state) or ni == 1
    row = pl.BlockSpec((tm, d), lambda i, j: (i, 0))
    mod = lambda arr: _mod_spec(n, tm, d, mod_rows, arr)
    in_specs = [row, pl.BlockSpec((1, d), lambda i, j: (0, 0)), mod(sc), mod(sh), mod(gate),
                pl.BlockSpec((d, tf), lambda i, j: (0, j)), pl.BlockSpec((d, tf), lambda i, j: (0, j + nff)),
                pl.BlockSpec((SUBLANES, tf), lambda i, j: (0, j)), pl.BlockSpec((SUBLANES, tf), lambda i, j: (0, j + nff)),
                pl.BlockSpec((tf, d), lambda i, j: (j, 0))]
    args = [x, gn, sc, sh, gate, w_up_b, w_up_b, conv_pack, conv_pack, w_down_b]
    scratch = [pltpu.VMEM((tm, d), BF16), pltpu.VMEM((tm, d), F32),
               pltpu.VMEM((tm + halo, tf), F32), pltpu.VMEM((tm + halo, tf), F32)]
    if from_state:
        in_specs += [pl.BlockSpec((halo, tf), lambda i, j: (0, j)), pl.BlockSpec((halo, tf), lambda i, j: (0, j + nff))]
        args += [state, state]
    else:
        scratch += [pltpu.VMEM((nff, halo, tf), F32), pltpu.VMEM((nff, halo, tf), F32)]
    tail_spec = pl.BlockSpec((1, halo, tf), lambda i, j: (i, 0, j))
    tail_shape = jax.ShapeDtypeStruct((ni, halo, D_FF), F32)
    return pl.pallas_call(
        functools.partial(_ffn_kernel, tm=tm, tf=tf, shift=shift, halo=halo,
                          tiles_per_seq=seq_rows // tm, from_state=from_state),
        grid=(ni, nff),
        in_specs=in_specs,
        out_specs=[row, tail_spec, tail_spec],
        out_shape=[jax.ShapeDtypeStruct((n, d), F32), tail_shape, tail_shape],
        scratch_shapes=scratch,
        compiler_params=_params(("arbitrary", "arbitrary"), 56),
        name="conv_ffn",
    )(*args)


def _final_norm_kernel(x_ref, g_ref, o_ref):
    x = x_ref[...]
    o_ref[...] = (x * lax.rsqrt(jnp.mean(x * x, axis=-1, keepdims=True) + EPS)) * g_ref[...]


def _final_norm(x, g, *, tm):
    n, d = x.shape
    return pl.pallas_call(
        _final_norm_kernel,
        grid=(n // tm,),
        in_specs=[pl.BlockSpec((tm, d), lambda i: (i, 0)), pl.BlockSpec((1, d), lambda i: (0, 0))],
        out_specs=pl.BlockSpec((tm, d), lambda i: (i, 0)),
        out_shape=jax.ShapeDtypeStruct((n, d), F32),
        compiler_params=_params(("arbitrary",), 32),
        name="final_norm",
    )(x, g.reshape(1, d))


def _prompt_bias_buckets(tq):
    c = np.arange(tq)[:, None]
    r = np.arange(tq)[None, :]
    diag = _t5_bucket_np(np.maximum(r - c, 0))
    prev = _t5_bucket_np(np.maximum(tq + r - c, 0))
    return np.concatenate([diag, prev], axis=1)


def _sample_bias_buckets(past, t_new, n_keys):
    t = np.minimum(np.arange(SUBLANES), t_new - 1)[:, None]
    j = np.arange(n_keys)[None, :]
    return _t5_bucket_np(np.maximum(past + t - j, 0))


def kernel(x_prompt, x_sample, c_prompt, c_sample, cache_k, cache_v, cache_idx_k, state_pool, state_conv,
           page_table, w_mod, b_mod, norm_attn_g, norm_ffn_g, w_in, w_pool, pool_scale, rel_bias, w_out,
           w_up, conv_w, conv_b, w_down, final_norm_g):
    bp, seq, d = x_prompt.shape
    nb, t_new, _ = x_sample.shape
    depth = w_in.shape[0]
    n_pages = page_table.shape[1]
    past = n_pages * PAGE_SIZE
    n_pool = cache_k.shape[1]
    ns = nb * t_new
    tq = 256
    tm_p, tm_s = 512, ns
    topk_p = min(TOPK_MAX, seq // 4)
    topk_s = min(TOPK_MAX, (past + t_new) // 4)

    w_in_b = w_in.astype(BF16)
    w_ki = w_in_b[:, :, COL_KW:COL_KW + IDX_DIM]
    w_aug = jnp.concatenate([w_in_b, jnp.zeros((depth, d, COL_KI2 - IN_WIDTH), BF16), w_ki, w_ki], axis=2)
    w_mod_b = w_mod.astype(BF16)
    w_pool_b = w_pool.astype(BF16)
    w_out_b = w_out.astype(BF16)
    w_up_b = w_up.astype(BF16)
    w_down_b = w_down.astype(BF16)
    conv_pack = jnp.concatenate([conv_w, conv_b[:, None, :], jnp.zeros((depth, SUBLANES - CONV_WIDTH - 1, 2 * D_FF), F32)], axis=1)

    cache_kb = cache_k.reshape(depth, n_pool, PAGE_SIZE, ATTN_WIDTH).astype(BF16)
    cache_vb = cache_v.reshape(depth, n_pool, PAGE_SIZE, ATTN_WIDTH).astype(BF16)
    cache_ib = jnp.pad(cache_idx_k.astype(BF16), ((0, 0), (0, 0), (0, 0), (0, LANES - IDX_DIM)))

    n_c = bp + nb
    c_all = jnp.concatenate([c_prompt, c_sample, jnp.zeros((-n_c % SUBLANES, d), F32)], axis=0)
    mod = _modulation(c_all, w_mod_b, b_mod)
    mod = mod.reshape(depth, mod.shape[1], 6, d)

    def mod_p(l, k):
        return jnp.broadcast_to(mod[l, :bp, k][:, None, :], (bp, SUBLANES, d)).reshape(bp * SUBLANES, d)

    def mod_s(l, k):
        return jnp.tile(mod[l, bp:bp + nb, k], (t_new, 1))

    bias_p = _bias_table(rel_bias, jnp.asarray(_prompt_bias_buckets(tq)))
    n_keys_s = past + LANES
    bias_s = _bias_table(rel_bias, jnp.asarray(_sample_bias_buckets(past, t_new, n_keys_s)))
    bias_s = bias_s.reshape(N_HEADS * SUBLANES, n_keys_s)

    xp = x_prompt.reshape(bp * seq, d)
    xs = jnp.swapaxes(x_sample, 0, 1).reshape(ns, d)

    def to_bt(a):
        return jnp.swapaxes(a.reshape(t_new, nb, a.shape[-1]), 0, 1)

    def pad_t(a, rows):
        return jnp.pad(a, ((0, 0), (0, rows - a.shape[1]), (0, 0)))

    outs_p = [[] for _ in range(5)]
    outs_s = [[] for _ in range(5)]
    for l in range(depth):
        g_attn = norm_attn_g[l].reshape(1, d)
        g_ffn = norm_ffn_g[l].reshape(1, d)

        u, q, k, v, kb, vt, qi, kw, ki2 = _inproj(xp, g_attn, mod_p(l, 1), mod_p(l, 0), w_aug[l], tm=tm_p, tk=tq, mod_rows=SUBLANES)
        pool = _pool_prompt(u, w_pool_b[l], pool_scale[l], tp=tm_p, seq=seq)
        sh3 = lambda a: a.reshape(bp, seq, a.shape[-1])
        attn = _attn_prompt(sh3(q), sh3(qi), sh3(kw), sh3(ki2), sh3(kb), vt, bias_p, tq=tq, topk=topk_p)
        x1 = _outproj(xp, pool, attn.reshape(bp * seq, ATTN_WIDTH), mod_p(l, 2), w_out_b[l], tm=tm_p, mod_rows=SUBLANES)
        xp, tail_a, tail_g = _ffn(x1, g_ffn, mod_p(l, 4), mod_p(l, 3), mod_p(l, 5), w_up_b[l], conv_pack[l], w_down_b[l],
                                  None, tm=tm_p, tf=1408, shift=1, halo=SUBLANES, seq_rows=seq, mod_rows=SUBLANES)
        tails = jnp.concatenate([tail_a, tail_g], axis=2).reshape(bp, seq // tm_p, SUBLANES, 2 * D_FF)
        outs_p[0].append(k.reshape(bp, seq, N_HEADS, HEAD_DIM))
        outs_p[1].append(v.reshape(bp, seq, N_HEADS, HEAD_DIM))
        outs_p[2].append(kw[:, :IDX_DIM].reshape(bp, seq, IDX_DIM))
        outs_p[3].append(u.reshape(bp, seq, POOL_WIDTH)[:, seq - POOL_STATE:])
        outs_p[4].append(tails[:, -1, SUBLANES - (CONV_WIDTH - 1):])

        u, q, k, v, kb, _, qi, kw, ki2 = _inproj(xs, g_attn, mod_s(l, 1), mod_s(l, 0), w_aug[l], tm=tm_s,
                                                 tk=tq if tm_s % tq == 0 else tm_s, mod_rows=tm_s)
        u_bt = to_bt(u)
        pool_state = jnp.concatenate([state_pool[l], u_bt], axis=1)
        pool = _pool_sample(jnp.swapaxes(pool_state, 0, 1), w_pool_b[l], pool_scale[l], t_new=t_new, pos0=past)
        pool = pool.reshape(ns, POOL_WIDTH)
        qi_bt = to_bt(qi.astype(F32)).reshape(nb, t_new, N_IDX_HEADS, IDX_DIM)
        qi_bd = jnp.pad(jnp.swapaxes(qi_bt, 1, 2), ((0, 0), (0, 0), (0, SUBLANES - t_new), (0, LANES - IDX_DIM)))
        qi_bd = qi_bd.reshape(nb, N_IDX_HEADS * SUBLANES, LANES).astype(BF16)
        kw_bt = to_bt(kw)
        wi_col = jnp.pad(jnp.swapaxes(kw_bt[:, :, IDX_DIM:IDX_DIM + N_IDX_HEADS], 1, 2), ((0, 0), (0, 0), (0, SUBLANES - t_new)))
        wi_col = wi_col.reshape(nb, N_IDX_HEADS * SUBLANES, 1)
        kin = pad_t(kw_bt[:, :, :IDX_DIM], NEW_ROWS)
        kin = jnp.pad(kin, ((0, 0), (0, 0), (0, LANES - IDX_DIM))).astype(BF16)
        off = _select_sample(page_table, qi_bd, wi_col, kin, cache_ib, layer=l, t_new=t_new, topk=topk_s)
        attn = _attn_sample(page_table, pad_t(to_bt(q.astype(F32)), SUBLANES), off,
                            pad_t(to_bt(kb), NEW_ROWS), pad_t(to_bt(v.astype(BF16)), NEW_ROWS), bias_s,
                            cache_kb, cache_vb, layer=l)
        attn = jnp.swapaxes(attn[:, :t_new], 0, 1).reshape(ns, ATTN_WIDTH).astype(BF16)
        x1 = _outproj(xs, pool, attn, mod_s(l, 2), w_out_b[l], tm=tm_s, mod_rows=tm_s)
        conv_prev = jnp.swapaxes(state_conv[l], 0, 1).reshape((CONV_WIDTH - 1) * nb, 2 * D_FF)
        xs, tail_a, tail_g = _ffn(x1, g_ffn, mod_s(l, 4), mod_s(l, 3), mod_s(l, 5), w_up_b[l], conv_pack[l], w_down_b[l],
                                  conv_prev, tm=tm_s, tf=1408, shift=nb, halo=(CONV_WIDTH - 1) * nb, seq_rows=ns, mod_rows=tm_s)
        conv_state = jnp.concatenate([tail_a, tail_g], axis=2).reshape(CONV_WIDTH - 1, nb, 2 * D_FF)
        outs_s[0].append(to_bt(k).reshape(nb, t_new, N_HEADS, HEAD_DIM))
        outs_s[1].append(to_bt(v).reshape(nb, t_new, N_HEADS, HEAD_DIM))
        outs_s[2].append(kw_bt[:, :, :IDX_DIM])
        outs_s[3].append(pool_state[:, -POOL_STATE:])
        outs_s[4].append(jnp.swapaxes(conv_state, 0, 1))

    y_prompt = _final_norm(xp, final_norm_g, tm=tm_p).reshape(bp, seq, d)
    y_sample = to_bt(_final_norm(xs, final_norm_g, tm=tm_s))
    return (y_prompt, y_sample) + tuple(jnp.stack(o) for o in outs_p) + tuple(jnp.stack(o) for o in outs_s)
```
